```python
import math
import jax, jax.numpy as jnp
from jax import lax
import numpy as np

D_MODEL = 1024
BATCH = 16
SEQ = 2048
DEPTH = 2
DEC_BATCH = 128
DEC_SEQ = 1
PAST_LEN = 8192
PAGE_SIZE = 128

N_EVEN = (DEPTH + 1) // 2
N_ODD = DEPTH // 2
HG_HEADS = 4
HG_DK = 128
HG_DV = 128
HG_CHUNK = 64
MLA_HEADS = 4
Q_LORA = 384
KV_LORA = 256
NOPE_DIM = 128
ROPE_DIM = 64
V_DIM = 128
ROPE_THETA = 10000.0
MLA_SCALE = (NOPE_DIM + ROPE_DIM) ** -0.5
Q_BLOCK = 128
CONV_DIM = D_MODEL
CONV_WIDTH = 31
PEER_HEADS = 8
N_KEYS = 128
N_EXPERTS = N_KEYS * N_KEYS
PEER_DK = 256
PEER_HALF = PEER_DK // 2
PEER_TOPK = 16
PEER_BLOCK = 256
DN_ALPHA = (2 * DEPTH) ** 0.25
DN_BETA = (8 * DEPTH) ** -0.25
LN_EPS = 1e-5
RMS_EPS = 1e-6
HG_QK_W = HG_HEADS * HG_DK
HG_V_W = HG_HEADS * HG_DV
MIX_IN_SIZES = (HG_QK_W, HG_QK_W, HG_V_W, HG_V_W, Q_LORA, KV_LORA, ROPE_DIM)
MIX_IN = sum(MIX_IN_SIZES)
SPLIT_POINTS = [int(v) for v in np.cumsum(MIX_IN_SIZES)[:-1]]
MIX_OUT = HG_V_W + MLA_HEADS * V_DIM

kernel_name = 'hybrid_hgrn2_mla_conformer_peer_step'


def layer_norm(x, g, b):
    xf = x.astype(jnp.float32)
    mu = jnp.mean(xf, axis=-1, keepdims=True)
    var = jnp.mean(jnp.square(xf - mu), axis=-1, keepdims=True)
    y = (xf - mu) * lax.rsqrt(var + LN_EPS) * g.astype(jnp.float32) + b.astype(jnp.float32)
    return y.astype(x.dtype)


def rms_norm(x, g):
    xf = x.astype(jnp.float32)
    y = xf * lax.rsqrt(jnp.mean(xf * xf, axis=-1, keepdims=True) + RMS_EPS) * g.astype(jnp.float32)
    return y.astype(x.dtype)


def rope(x, pos):
    half = ROPE_DIM // 2
    inv_freq = ROPE_THETA ** (-jnp.arange(half, dtype=jnp.float32) / half)
    ang = pos.astype(jnp.float32)[:, None] * inv_freq[None, :]
    cos = jnp.cos(ang)[:, None, :]
    sin = jnp.sin(ang)[:, None, :]
    xf = x.astype(jnp.float32)
    x1, x2 = xf[..., :half], xf[..., half:]
    return jnp.concatenate([x1 * cos - x2 * sin, x1 * sin + x2 * cos], axis=-1).astype(x.dtype)


def hgrn2_recurrence(q, k, v, log_f, s0):
    B, L, H, _ = q.shape
    DV = v.shape[-1]
    C = min(HG_CHUNK, L)
    n = -(-L // C)
    pad = n * C - L

    def blocks(t):
        t = jnp.pad(t, ((0, 0), (0, pad), (0, 0), (0, 0)))
        return t.reshape(B, n, C, H, t.shape[-1]).transpose(1, 0, 3, 2, 4)

    tri = jnp.tril(jnp.ones((C, C), dtype=bool))

    def step(S, inp):
        qc, kc, vc, gc = inp
        b = jnp.cumsum(gc, axis=2)
        o_inter = jnp.einsum('bhtk,bhkv->bhtv', qc * jnp.exp(b), S)
        diff = b[:, :, :, None, :] - b[:, :, None, :, :]
        decay = jnp.exp(jnp.where(tri[:, :, None], diff, -jnp.inf))
        attn = jnp.einsum('bhtk,bhtsk,bhsk->bhts', qc, decay, kc)
        o_intra = jnp.einsum('bhts,bhsv->bhtv', attn, vc)
        b_last = b[:, :, -1:, :]
        S_new = jnp.exp(b_last[:, :, 0, :, None]) * S + jnp.einsum('bhsk,bhsv->bhkv', kc * jnp.exp(b_last - b), vc)
        return S_new, o_inter + o_intra

    S_fin, o = lax.scan(step, s0, (blocks(q), blocks(k), blocks(v), blocks(log_f)))
    o = o.transpose(1, 0, 3, 2, 4).reshape(B, n * C, H, DV)[:, :L]
    return o, S_fin


def latent_scores(q_abs, q_pe, c_kv, k_pe):
    s = jnp.einsum('bqhc,bkc->bhqk', q_abs, c_kv) + jnp.einsum('bqhr,bkr->bhqk', q_pe, k_pe)
    return s.astype(jnp.float32) * MLA_SCALE


def mla_prompt_attention(q_abs, q_pe, c_kv, k_pe):
    B, L, H, _ = q_abs.shape
    blk = Q_BLOCK if L % Q_BLOCK == 0 else L
    nb = L // blk
    k_idx = jnp.arange(L)

    def split(t):
        return t.reshape(B, nb, blk, H, t.shape[-1]).transpose(1, 0, 2, 3, 4)

    def block(args):
        qa, qp, q0 = args
        s = latent_scores(qa, qp, c_kv, k_pe)
        q_idx = q0 + jnp.arange(blk)
        s = jnp.where(k_idx[None, :] <= q_idx[:, None], s, -jnp.inf)
        p = jax.nn.softmax(s, axis=-1).astype(c_kv.dtype)
        return jnp.einsum('bhqk,bkc->bqhc', p, c_kv)

    o = lax.map(block, (split(q_abs), split(q_pe), jnp.arange(nb) * blk))
    return o.transpose(1, 0, 2, 3, 4).reshape(B, L, H, KV_LORA)


def mla_sample_attention(q_abs, q_pe, c_new, kpe_new, c_past, kpe_past):
    Lq = q_abs.shape[1]
    s_past = latent_scores(q_abs, q_pe, c_past, kpe_past)
    s_new = latent_scores(q_abs, q_pe, c_new, kpe_new)
    tri = jnp.tril(jnp.ones((Lq, Lq), dtype=bool))
    s_new = jnp.where(tri, s_new, -jnp.inf)
    P = s_past.shape[-1]
    p = jax.nn.softmax(jnp.concatenate([s_past, s_new], axis=-1), axis=-1)
    o = jnp.einsum('bhqk,bkc->bqhc', p[..., :P].astype(c_past.dtype), c_past)
    o = o + jnp.einsum('bhqk,bkc->bqhc', p[..., P:].astype(c_new.dtype), c_new)
    return o.astype(q_abs.dtype)


def hybrid_mixer(x, pos, lb, w_in, q_norm, w_uq, kv_norm, w_ukv, out_norm, w_out, s0, c_past, kpe_past):
    B, L, _ = x.shape
    proj = x @ w_in
    q_hg, f_hg, i_hg, g_hg, cq, ckv, kr = jnp.split(proj, SPLIT_POINTS, axis=-1)
    z = f_hg.astype(jnp.float32).reshape(B, L, HG_HEADS, HG_DK)
    log_f = jnp.log(lb + (1.0 - lb) * jax.nn.sigmoid(z))
    k_in = (1.0 - lb) * jax.nn.sigmoid(-z)
    o_hg, s_new = hgrn2_recurrence(q_hg.astype(jnp.float32).reshape(B, L, HG_HEADS, HG_DK), k_in,
                                   i_hg.astype(jnp.float32).reshape(B, L, HG_HEADS, HG_DV), log_f,
                                   s0.astype(jnp.float32))
    o_hg = rms_norm(o_hg, out_norm) * jax.nn.silu(g_hg.astype(jnp.float32).reshape(B, L, HG_HEADS, HG_DV))
    o_hg = o_hg.reshape(B, L, HG_V_W).astype(x.dtype)
    qh = (rms_norm(cq, q_norm) @ w_uq).reshape(B, L, MLA_HEADS, NOPE_DIM + ROPE_DIM)
    q_nope = qh[..., :NOPE_DIM]
    q_pe = rope(qh[..., NOPE_DIM:], pos)
    c_kv = rms_norm(ckv, kv_norm)
    k_pe = rope(kr[:, :, None, :], pos)[:, :, 0, :]
    q_abs = jnp.einsum('blhn,chn->blhc', q_nope, w_ukv[..., :NOPE_DIM])
    if c_past is None:
        o_lat = mla_prompt_attention(q_abs, q_pe, c_kv, k_pe)
    else:
        o_lat = mla_sample_attention(q_abs, q_pe, c_kv, k_pe, c_past, kpe_past)
    o_mla = jnp.einsum('blhc,chv->blhv', o_lat, w_ukv[..., NOPE_DIM:]).reshape(B, L, MLA_HEADS * V_DIM)
    out = jnp.concatenate([o_hg, o_mla.astype(x.dtype)], axis=-1) @ w_out
    return out, c_kv, k_pe, s_new


def conv_module(x, buf, w_pw1, b_pw1, w_dw, b_dw, ln_g, ln_b, w_pw2, b_pw2):
    B = x.shape[0]
    h = x @ w_pw1 + b_pw1
    u = h[..., :CONV_DIM] * jax.nn.sigmoid(h[..., CONV_DIM:])
    if buf is None:
        buf = jnp.zeros((B, CONV_WIDTH - 1, CONV_DIM), u.dtype)
    ext = jnp.concatenate([buf.astype(u.dtype), u], axis=1)
    y = lax.conv_general_dilated(ext, w_dw[:, None, :].astype(u.dtype), (1,), 'VALID',
                                 dimension_numbers=('NWC', 'WIO', 'NWC'),
                                 feature_group_count=CONV_DIM) + b_dw
    y = jax.nn.silu(layer_norm(y, ln_g, ln_b))
    return y @ w_pw2 + b_pw2, ext[:, ext.shape[1] - (CONV_WIDTH - 1):]


def peer_ffn(x, w_q, sub_keys, u_tab, v_tab):
    B, L, D = x.shape
    n = B * L
    blk = min(PEER_BLOCK, n)
    nb = -(-n // blk)
    xp = jnp.pad(x.reshape(n, D), ((0, nb * blk - n), (0, 0))).reshape(nb, blk, D)

    def block(xb):
        q = (xb @ w_q).reshape(blk, PEER_HEADS, 2, PEER_HALF)
        s = jnp.einsum('thpd,phkd->thpk', q, sub_keys).astype(jnp.float32)
        sv, si = lax.top_k(s, PEER_TOPK)
        cand = sv[:, :, 0, :, None] + sv[:, :, 1, None, :]
        cand_idx = si[:, :, 0, :, None] * N_KEYS + si[:, :, 1, None, :]
        top_s, top_pos = lax.top_k(cand.reshape(blk, PEER_HEADS, PEER_TOPK * PEER_TOPK), PEER_TOPK)
        idx = jnp.take_along_axis(cand_idx.reshape(blk, PEER_HEADS, PEER_TOPK * PEER_TOPK), top_pos, axis=-1)
        g = jax.nn.softmax(top_s, axis=-1)
        u = u_tab[idx]
        v = v_tab[idx]
        act = jax.nn.gelu(jnp.einsum('thkd,td->thk', u, xb).astype(jnp.float32), approximate=False)
        return jnp.einsum('thk,thkd->td', (g * act).astype(xb.dtype), v)

    y = lax.map(block, xp).reshape(nb * blk, D)[:n]
    return y.reshape(B, L, D)


def setup_inputs(seed: int = 0) -> dict:
    key = jax.random.key(seed)
    keys = jax.random.split(key, 40)
    counter = [0]

    def nxt():
        k = keys[counter[0]]
        counter[0] += 1
        return k

    def nrm(shape, scale):
        return jax.random.normal(nxt(), shape, jnp.float32) * scale

    n_pages = PAST_LEN // PAGE_SIZE
    n_used = DEC_BATCH * n_pages
    n_pool = (5 * n_used + 3) // 4
    inp = {}
    inp['x_prompt'] = nrm((BATCH, SEQ, D_MODEL), 1.0)
    inp['x_sample'] = nrm((DEC_BATCH, DEC_SEQ, D_MODEL), 1.0)
    inp['cache_mla_latent'] = nrm((N_EVEN, n_pool, PAGE_SIZE, KV_LORA), 1.0)
    inp['cache_mla_krope'] = nrm((N_EVEN, n_pool, PAGE_SIZE, ROPE_DIM), 1.0)
    inp['state_hgrn'] = nrm((N_EVEN, DEC_BATCH, HG_HEADS, HG_DK, HG_DV), 0.5)
    inp['state_conv'] = nrm((N_ODD, DEC_BATCH, CONV_WIDTH - 1, CONV_DIM), 1.0)
    perm = jax.random.permutation(nxt(), n_pool)
    inp['page_table'] = perm[:n_used].reshape(DEC_BATCH, n_pages).astype(jnp.int32)
    inp['hg_lb_logits'] = nrm((DEPTH + 1, HG_QK_W), 0.1)
    inp['mix_w_in'] = nrm((N_EVEN, D_MODEL, MIX_IN), D_MODEL ** -0.5)
    inp['mla_q_norm'] = 1.0 + nrm((N_EVEN, Q_LORA), 0.02)
    inp['mla_w_uq'] = nrm((N_EVEN, Q_LORA, MLA_HEADS * (NOPE_DIM + ROPE_DIM)), Q_LORA ** -0.5)
    inp['mla_kv_norm'] = 1.0 + nrm((N_EVEN, KV_LORA), 0.02)
    inp['mla_w_ukv'] = nrm((N_EVEN, KV_LORA, MLA_HEADS, NOPE_DIM + V_DIM), KV_LORA ** -0.5)
    inp['hg_out_norm'] = 1.0 + nrm((N_EVEN, HG_DV), 0.02)
    inp['mix_w_out'] = nrm((N_EVEN, MIX_OUT, D_MODEL), DN_BETA * MIX_OUT ** -0.5)
    inp['conv_w_pw1'] = nrm((N_ODD, D_MODEL, 2 * CONV_DIM), D_MODEL ** -0.5)
    inp['conv_b_pw1'] = nrm((N_ODD, 2 * CONV_DIM), 0.02)
    inp['conv_w_dw'] = nrm((N_ODD, CONV_WIDTH, CONV_DIM), CONV_WIDTH ** -0.5)
    inp['conv_b_dw'] = nrm((N_ODD, CONV_DIM), 0.02)
    inp['conv_ln_g'] = 1.0 + nrm((N_ODD, CONV_DIM), 0.02)
    inp['conv_ln_b'] = nrm((N_ODD, CONV_DIM), 0.02)
    inp['conv_w_pw2'] = nrm((N_ODD, CONV_DIM, D_MODEL), DN_BETA * CONV_DIM ** -0.5)
    inp['conv_b_pw2'] = nrm((N_ODD, D_MODEL), 0.02)
    inp['peer_w_q'] = nrm((DEPTH, D_MODEL, PEER_HEADS * PEER_DK), D_MODEL ** -0.5)
    inp['peer_sub_keys'] = nrm((DEPTH, 2, PEER_HEADS, N_KEYS, PEER_HALF), PEER_HALF ** -0.5)
    inp['peer_u'] = nrm((DEPTH, N_EXPERTS, D_MODEL), D_MODEL ** -0.5)
    inp['peer_v'] = nrm((DEPTH, N_EXPERTS, D_MODEL), DN_BETA * PEER_HEADS ** -0.5)
    inp['ln_g'] = 1.0 + nrm((DEPTH, 2, D_MODEL), 0.02)
    inp['ln_b'] = nrm((DEPTH, 2, D_MODEL), 0.02)
    return inp


def reference(x_prompt, x_sample, cache_mla_latent, cache_mla_krope, state_hgrn, state_conv, page_table,
              hg_lb_logits, mix_w_in, mla_q_norm, mla_w_uq, mla_kv_norm, mla_w_ukv, hg_out_norm, mix_w_out,
              conv_w_pw1, conv_b_pw1, conv_w_dw, conv_b_dw, conv_ln_g, conv_ln_b, conv_w_pw2, conv_b_pw2,
              peer_w_q, peer_sub_keys, peer_u, peer_v, ln_g, ln_b):
    bp, lp, _ = x_prompt.shape
    bs, ls, _ = x_sample.shape
    n_pages = page_table.shape[1]
    past_len = n_pages * PAGE_SIZE
    pos_p = jnp.arange(lp)
    pos_s = past_len + jnp.arange(ls)
    lb_all = jnp.cumsum(jax.nn.softmax(hg_lb_logits.astype(jnp.float32), axis=0), axis=0)
    hp, hs = x_prompt, x_sample
    lat_p, kpe_p, hg_p, conv_p = [], [], [], []
    lat_s, kpe_s, hg_s, conv_s = [], [], [], []
    for layer in range(DEPTH):
        if layer % 2 == 0:
            e = layer // 2
            lb = lb_all[layer].reshape(HG_HEADS, HG_DK)
            w = (mix_w_in[e], mla_q_norm[e], mla_w_uq[e], mla_kv_norm[e], mla_w_ukv[e], hg_out_norm[e], mix_w_out[e])
            s0 = jnp.zeros((bp, HG_HEADS, HG_DK, HG_DV), jnp.float32)
            mp, c_new_p, k_new_p, s_new_p = hybrid_mixer(hp, pos_p, lb, *w, s0, None, None)
            c_past = cache_mla_latent[e][page_table].reshape(bs, past_len, KV_LORA)
            k_past = cache_mla_krope[e][page_table].reshape(bs, past_len, ROPE_DIM)
            ms, c_new_s, k_new_s, s_new_s = hybrid_mixer(hs, pos_s, lb, *w, state_hgrn[e], c_past, k_past)
            lat_p.append(c_new_p)
            kpe_p.append(k_new_p)
            hg_p.append(s_new_p)
            lat_s.append(c_new_s)
            kpe_s.append(k_new_s)
            hg_s.append(s_new_s)
        else:
            o = layer // 2
            w = (conv_w_pw1[o], conv_b_pw1[o], conv_w_dw[o], conv_b_dw[o], conv_ln_g[o], conv_ln_b[o],
                 conv_w_pw2[o], conv_b_pw2[o])
            mp, buf_p = conv_module(hp, None, *w)
            ms, buf_s = conv_module(hs, state_conv[o], *w)
            conv_p.append(buf_p)
            conv_s.append(buf_s)
        hp = layer_norm(DN_ALPHA * hp + mp, ln_g[layer, 0], ln_b[layer, 0])
        hs = layer_norm(DN_ALPHA * hs + ms, ln_g[layer, 0], ln_b[layer, 0])
        pw = (peer_w_q[layer], peer_sub_keys[layer], peer_u[layer], peer_v[layer])
        hp = layer_norm(DN_ALPHA * hp + peer_ffn(hp, *pw), ln_g[layer, 1], ln_b[layer, 1])
        hs = layer_norm(DN_ALPHA * hs + peer_ffn(hs, *pw), ln_g[layer, 1], ln_b[layer, 1])
    new_lat_p = jnp.stack(lat_p)
    new_kpe_p = jnp.stack(kpe_p)
    new_hgrn_p = jnp.stack(hg_p)
    new_conv_p = jnp.stack(conv_p)
    new_lat_s = jnp.stack(lat_s)
    new_kpe_s = jnp.stack(kpe_s)
    new_hgrn_s = jnp.stack(hg_s)
    new_conv_s = jnp.stack(conv_s)
    return (hp, hs, new_lat_p, new_kpe_p, new_hgrn_p, new_conv_p, new_lat_s, new_kpe_s, new_hgrn_s, new_conv_s)
```

```python
import functools
import math

import jax
import jax.numpy as jnp
from jax import lax
from jax.experimental import pallas as pl
from jax.experimental.pallas import tpu as pltpu

D_MODEL = 1024
DEPTH = 2
PAGE_SIZE = 128
HG_HEADS = 4
HG_DK = 128
HG_DV = 128
HG_CHUNK = 64
HG_SUB = 16
MLA_HEADS = 4
Q_LORA = 384
KV_LORA = 256
NOPE_DIM = 128
ROPE_DIM = 64
ROPE_HALF = ROPE_DIM // 2
V_DIM = 128
ROPE_THETA = 10000.0
MLA_SCALE = (NOPE_DIM + ROPE_DIM) ** -0.5
CONV_DIM = D_MODEL
CONV_WIDTH = 31
PEER_HEADS = 8
N_KEYS = 128
N_EXPERTS = N_KEYS * N_KEYS
PEER_HALF = 128
PEER_TOPK = 16
DN_ALPHA = (2 * DEPTH) ** 0.25
LN_EPS = 1e-5
RMS_EPS = 1e-6
HG_QK_W = HG_HEADS * HG_DK
HG_V_W = HG_HEADS * HG_DV
HG_IN = 2 * HG_QK_W + 2 * HG_V_W
MLA_KW = KV_LORA + 2 * 128

LANES = 128
VMEM_LIMIT = 56 * 1024 * 1024

_BF = jnp.bfloat16
_F32 = jnp.float32
_NT = (((1,), (1,)), ((), ()))
_TN = (((0,), (0,)), ((), ()))
_NEG_INF = float("-inf")

_PEER_CELLS = tuple((a, b) for a in range(PEER_TOPK) for b in range(PEER_TOPK) if (a + 1) * (b + 1) <= PEER_TOPK)


def _params(*sem):
    return pltpu.CompilerParams(dimension_semantics=sem, vmem_limit_bytes=VMEM_LIMIT)


def _linear_kernel(*refs, n_in, has_bias):
    o_ref = refs[-1]
    acc = None
    for x_ref, w_ref in zip(refs[:n_in], refs[n_in:2 * n_in]):
        d = jnp.dot(x_ref[...].astype(_BF), w_ref[...], preferred_element_type=_F32)
        acc = d if acc is None else acc + d
    if has_bias:
        acc = acc + refs[2 * n_in][...]
    o_ref[...] = acc


def _linear(xs, ws, bias=None, tm=512):
    m = xs[0].shape[0]
    n = ws[0].shape[1]
    tm = min(tm, m)
    in_specs = [pl.BlockSpec((tm, x.shape[1]), lambda i: (i, 0)) for x in xs]
    in_specs += [pl.BlockSpec(w.shape, lambda i: (0, 0)) for w in ws]
    args = list(xs) + list(ws)
    if bias is not None:
        in_specs.append(pl.BlockSpec((1, n), lambda i: (0, 0)))
        args.append(bias.reshape(1, n))
    return pl.pallas_call(
        functools.partial(_linear_kernel, n_in=len(xs), has_bias=bias is not None),
        grid=(m // tm,),
        in_specs=in_specs,
        out_specs=pl.BlockSpec((tm, n), lambda i: (i, 0)),
        out_shape=jax.ShapeDtypeStruct((m, n), _F32),
        compiler_params=_params("parallel"),
        name="linear",
    )(*args)


def _glu_kernel(x_ref, wa_ref, wb_ref, ba_ref, bb_ref, o_ref):
    xb = x_ref[...].astype(_BF)
    a = jnp.dot(xb, wa_ref[...], preferred_element_type=_F32) + ba_ref[...]
    g = jnp.dot(xb, wb_ref[...], preferred_element_type=_F32) + bb_ref[...]
    o_ref[...] = a * jax.nn.sigmoid(g)


def _glu_linear(x, w, b, tm=512):
    m, k = x.shape
    n = w.shape[1] // 2
    tm = min(tm, m)
    b2 = b.reshape(1, 2 * n)
    return pl.pallas_call(
        _glu_kernel,
        grid=(m // tm,),
        in_specs=[pl.BlockSpec((tm, k), lambda i: (i, 0)),
                  pl.BlockSpec((k, n), lambda i: (0, 0)),
                  pl.BlockSpec((k, n), lambda i: (0, 1)),
                  pl.BlockSpec((1, n), lambda i: (0, 0)),
                  pl.BlockSpec((1, n), lambda i: (0, 1))],
        out_specs=pl.BlockSpec((tm, n), lambda i: (i, 0)),
        out_shape=jax.ShapeDtypeStruct((m, n), _F32),
        compiler_params=_params("parallel"),
        name="glu_linear",
    )(x, w, w, b2, b2)


def _layer_norm_rows(v, g, b):
    mu = jnp.mean(v, axis=-1, keepdims=True)
    c = v - mu
    var = jnp.mean(c * c, axis=-1, keepdims=True)
    return c * lax.rsqrt(var + LN_EPS) * g + b


def _add_ln_kernel(x_ref, f_ref, g_ref, b_ref, o_ref):
    o_ref[...] = _layer_norm_rows(DN_ALPHA * x_ref[...] + f_ref[...], g_ref[...], b_ref[...])


def _add_ln(x, f, g, b, tm=512):
    m, d = x.shape
    tm = min(tm, m)
    row = pl.BlockSpec((tm, d), lambda i: (i, 0))
    vec = pl.BlockSpec((1, d), lambda i: (0, 0))
    return pl.pallas_call(
        _add_ln_kernel,
        grid=(m // tm,),
        in_specs=[row, row, vec, vec],
        out_specs=row,
        out_shape=jax.ShapeDtypeStruct((m, d), _F32),
        compiler_params=_params("parallel"),
        name="add_ln",
    )(x, f, g.reshape(1, d), b.reshape(1, d))


def _hg_out(o, gate, norm_row):
    y = o * lax.rsqrt(jnp.mean(o * o, axis=-1, keepdims=True) + RMS_EPS) * norm_row
    return y * (gate * jax.nn.sigmoid(gate))


def _hgrn_prompt_kernel(q_ref, f_ref, i_ref, g_ref, lb_ref, norm_ref, o_ref, s_ref, st_sc):
    c = HG_CHUNK
    step = pl.program_id(1)

    @pl.when(step == 0)
    def _():
        st_sc[...] = jnp.zeros_like(st_sc)

    row = lax.broadcasted_iota(jnp.int32, (c, c), 0)
    col = lax.broadcasted_iota(jnp.int32, (c, c), 1)
    tril = (col <= row).astype(_F32)
    sub_row = lax.broadcasted_iota(jnp.int32, (HG_SUB, HG_DK), 0)
    norm_row = norm_ref[...]

    for h in range(HG_HEADS):
        sl = slice(h * HG_DK, (h + 1) * HG_DK)
        lb = lb_ref[:, sl]
        q = q_ref[0, :, sl]
        z = f_ref[0, :, sl]
        v = i_ref[0, :, sl]
        log_f = jnp.log(lb + (1.0 - lb) * jax.nn.sigmoid(z))
        k = (1.0 - lb) * jax.nn.sigmoid(-z)
        b = jnp.dot(tril, log_f, preferred_element_type=_F32, precision=lax.Precision.HIGHEST)
        b_last = b[c - 1:c, :]
        st = st_sc[h]
        v_bf = v.astype(_BF)
        o = lax.dot_general((q * jnp.exp(b)).astype(_BF), st.astype(_BF), _NT, preferred_element_type=_F32)
        pieces = []
        for blk in range(c // HG_SUB):
            r0 = blk * HG_SUB
            rs = slice(r0, r0 + HG_SUB)
            qb, kb, vb, bb = q[rs], k[rs], v[rs], b[rs]
            ob = o[rs]
            if blk > 0:
                ref_row = b[r0 - 1:r0, :]
                qd = (qb * jnp.exp(bb - ref_row)).astype(_BF)
                kd = (k[:r0] * jnp.exp(ref_row - b[:r0])).astype(_BF)
                a_off = lax.dot_general(qd, kd, _NT, preferred_element_type=_F32)
                ob = ob + jnp.dot(a_off.astype(_BF), v_bf[:r0], preferred_element_type=_F32)
            for s in range(HG_SUB):
                w = jnp.where(sub_row >= s, jnp.exp(bb - bb[s:s + 1, :]), 0.0)
                a_col = jnp.sum(qb * kb[s:s + 1, :] * w, axis=-1, keepdims=True)
                ob = ob + a_col * vb[s:s + 1, :]
            pieces.append(ob)
        o_full = jnp.concatenate(pieces, axis=0)
        o_ref[0, :, sl] = _hg_out(o_full, g_ref[0, :, sl], norm_row)
        kd_all = (k * jnp.exp(b_last - b)).astype(_BF)
        st_new = st * jnp.exp(b_last) + jnp.dot(v.T.astype(_BF), kd_all, preferred_element_type=_F32)
        st_sc[h] = st_new

        @pl.when(step == pl.num_programs(1) - 1)
        def _():
            s_ref[0, h] = st_new.T


def _hgrn_prompt(hg, lb, out_norm):
    bsz, seq, _ = hg.shape
    c = HG_CHUNK
    w = HG_QK_W
    blk = lambda j: pl.BlockSpec((1, c, w), lambda b, l, j=j: (b, l, j))
    return pl.pallas_call(
        _hgrn_prompt_kernel,
        grid=(bsz, seq // c),
        in_specs=[blk(0), blk(1), blk(2), blk(3),
                  pl.BlockSpec((1, w), lambda b, l: (0, 0)),
                  pl.BlockSpec((1, HG_DV), lambda b, l: (0, 0))],
        out_specs=[pl.BlockSpec((1, c, HG_V_W), lambda b, l: (b, l, 0)),
                   pl.BlockSpec((1, HG_HEADS, HG_DK, HG_DV), lambda b, l: (b, 0, 0, 0))],
        out_shape=[jax.ShapeDtypeStruct((bsz, seq, HG_V_W), _F32),
                   jax.ShapeDtypeStruct((bsz, HG_HEADS, HG_DK, HG_DV), _F32)],
        scratch_shapes=[pltpu.VMEM((HG_HEADS, HG_DV, HG_DK), _F32)],
        compiler_params=_params("parallel", "arbitrary"),
        name="hgrn_prompt",
    )(hg, hg, hg, hg, lb.reshape(1, w), out_norm.reshape(1, HG_DV))


def _hgrn_step_kernel(cols_ref, row_ref, lbc_ref, norm_ref, s_ref, o_ref, s_out_ref):
    norm_row = norm_ref[...]
    for h in range(HG_HEADS):
        sl_v = slice(2 * HG_QK_W + h * HG_DV, 2 * HG_QK_W + (h + 1) * HG_DV)
        sl_g = slice(2 * HG_QK_W + HG_V_W + h * HG_DV, 2 * HG_QK_W + HG_V_W + (h + 1) * HG_DV)
        qc = cols_ref[0, h, :, 0:1]
        zc = cols_ref[0, h, :, 1:2]
        lbc = lbc_ref[h]
        f = lbc + (1.0 - lbc) * jax.nn.sigmoid(zc)
        kc = (1.0 - lbc) * jax.nn.sigmoid(-zc)
        v_row = row_ref[0, :, sl_v]
        s_new = f * s_ref[0, h] + kc * v_row
        s_out_ref[0, h] = s_new
        o = jnp.sum(qc * s_new, axis=0, keepdims=True)
        o_ref[0, :, h * HG_DV:(h + 1) * HG_DV] = _hg_out(o, row_ref[0, :, sl_g], norm_row)


def _hgrn_step(hg, state, lb, out_norm):
    bsz = hg.shape[0]
    cols = hg[:, :2 * HG_QK_W].reshape(bsz, 2, HG_HEADS, HG_DK).transpose(0, 2, 3, 1)
    rows = hg.reshape(bsz, 1, HG_IN)
    lbc = lb.reshape(HG_HEADS, HG_DK, 1)
    st_spec = pl.BlockSpec((1, HG_HEADS, HG_DK, HG_DV), lambda b: (b, 0, 0, 0))
    o, s_new = pl.pallas_call(
        _hgrn_step_kernel,
        grid=(bsz,),
        in_specs=[pl.BlockSpec((1, HG_HEADS, HG_DK, 2), lambda b: (b, 0, 0, 0)),
                  pl.BlockSpec((1, 1, HG_IN), lambda b: (b, 0, 0)),
                  pl.BlockSpec((HG_HEADS, HG_DK, 1), lambda b: (0, 0, 0)),
                  pl.BlockSpec((1, HG_DV), lambda b: (0, 0)),
                  st_spec],
        out_specs=[pl.BlockSpec((1, 1, HG_V_W), lambda b: (b, 0, 0)), st_spec],
        out_shape=[jax.ShapeDtypeStruct((bsz, 1, HG_V_W), _F32),
                   jax.ShapeDtypeStruct(state.shape, _F32)],
        compiler_params=_params("parallel"),
        name="hgrn_step",
    )(cols, rows, lbc, out_norm.reshape(1, HG_DV), state)
    return o.reshape(bsz, HG_V_W), s_new


def _rms_rows(x, g):
    return x * lax.rsqrt(jnp.mean(x * x, axis=-1, keepdims=True) + RMS_EPS) * g


def _mla_prep_kernel(x_ref, win_ref, qn_ref, wq_ref, kvn_ref, wk_ref, cos_ref, sin_ref, hm_ref,
                     qcat_ref, kcat_ref, ckv_ref, kpe_ref):
    m = jnp.dot(x_ref[0].astype(_BF), win_ref[...], preferred_element_type=_F32)
    cq = _rms_rows(m[:, :Q_LORA], qn_ref[...])
    c_kv = _rms_rows(m[:, Q_LORA:Q_LORA + KV_LORA], kvn_ref[...])
    kr1 = m[:, Q_LORA + KV_LORA:Q_LORA + KV_LORA + LANES]
    kr2 = m[:, Q_LORA + KV_LORA + LANES:]
    cos = cos_ref[...]
    sin = sin_ref[...]
    k1 = kr1 * cos - kr2 * sin
    k2 = kr1 * sin + kr2 * cos
    ckv_ref[0] = c_kv
    lane = lax.broadcasted_iota(jnp.int32, k1.shape, 1)
    kpe_ref[0] = jnp.where(lane < ROPE_HALF, k1, k2)[:, :ROPE_DIM]
    kcat_ref[0, :, :KV_LORA] = c_kv.astype(_BF)
    kcat_ref[0, :, KV_LORA:KV_LORA + LANES] = k1.astype(_BF)
    kcat_ref[0, :, KV_LORA + LANES:] = k2.astype(_BF)

    qh = jnp.dot(cq.astype(_BF), wq_ref[...], preferred_element_type=_F32)
    nope_w = MLA_HEADS * NOPE_DIM
    q1 = qh[:, nope_w:nope_w + LANES]
    q2 = qh[:, nope_w + LANES:]
    r1 = q1 * cos - q2 * sin
    r2 = q1 * sin + q2 * cos
    for h in range(MLA_HEADS):
        q_nope = qh[:, h * NOPE_DIM:(h + 1) * NOPE_DIM].astype(_BF)
        q_abs = jnp.dot(q_nope, wk_ref[h], preferred_element_type=_F32)
        hm = hm_ref[h:h + 1, :]
        qcat_ref[0, h, :, :KV_LORA] = q_abs.astype(_BF)
        qcat_ref[0, h, :, KV_LORA:KV_LORA + LANES] = (r1 * hm).astype(_BF)
        qcat_ref[0, h, :, KV_LORA + LANES:] = (r2 * hm).astype(_BF)


def _mla_prep(x, w_in, q_norm, w_q, kv_norm, w_k, cos, sin, head_mask, tm=256):
    bsz, seq, d = x.shape
    tm = min(tm, seq)
    full = lambda a: pl.BlockSpec(a.shape, lambda b, l: (0,) * a.ndim)
    qn = q_norm.reshape(1, Q_LORA)
    kvn = kv_norm.reshape(1, KV_LORA)
    return pl.pallas_call(
        _mla_prep_kernel,
        grid=(bsz, seq // tm),
        in_specs=[pl.BlockSpec((1, tm, d), lambda b, l: (b, l, 0)),
                  full(w_in), full(qn), full(w_q), full(kvn), full(w_k),
                  pl.BlockSpec((tm, LANES), lambda b, l: (l, 0)),
                  pl.BlockSpec((tm, LANES), lambda b, l: (l, 0)),
                  full(head_mask)],
        out_specs=[pl.BlockSpec((1, MLA_HEADS, tm, MLA_KW), lambda b, l: (b, 0, l, 0)),
                   pl.BlockSpec((1, tm, MLA_KW), lambda b, l: (b, l, 0)),
                   pl.BlockSpec((1, tm, KV_LORA), lambda b, l: (b, l, 0)),
                   pl.BlockSpec((1, tm, ROPE_DIM), lambda b, l: (b, l, 0))],
        out_shape=[jax.ShapeDtypeStruct((bsz, MLA_HEADS, seq, MLA_KW), _BF),
                   jax.ShapeDtypeStruct((bsz, seq, MLA_KW), _BF),
                   jax.ShapeDtypeStruct((bsz, seq, KV_LORA), _F32),
                   jax.ShapeDtypeStruct((bsz, seq, ROPE_DIM), _F32)],
        compiler_params=_params("parallel", "parallel"),
        name="mla_prep",
    )(x, w_in, qn, w_q, kvn, w_k, cos, sin, head_mask)


def _mla_attn_kernel(q_ref, k_ref, wv_ref, o_ref, m_sc, l_sc, acc_sc, *, tq, tk):
    qi = pl.program_id(1)
    kj = pl.program_id(2)
    rows = MLA_HEADS * tq

    @pl.when(kj == 0)
    def _():
        m_sc[...] = jnp.full_like(m_sc, _NEG_INF)
        l_sc[...] = jnp.zeros_like(l_sc)
        acc_sc[...] = jnp.zeros_like(acc_sc)

    @pl.when(kj * tk <= qi * tq + (tq - 1))
    def _():
        q = q_ref[0].reshape(rows, MLA_KW)
        k = k_ref[0]
        s = lax.dot_general(q, k, _NT, preferred_element_type=_F32) * MLA_SCALE
        q_pos = qi * tq + (lax.broadcasted_iota(jnp.int32, (rows, tk), 0) & (tq - 1))
        k_pos = kj * tk + lax.broadcasted_iota(jnp.int32, (rows, tk), 1)
        s = jnp.where(k_pos <= q_pos, s, _NEG_INF)
        m_old = m_sc[...]
        m_new = jnp.maximum(m_old, jnp.max(s, axis=-1, keepdims=True))
        alpha = jnp.exp(m_old - m_new)
        p = jnp.exp(s - m_new)
        l_sc[...] = alpha * l_sc[...] + jnp.sum(p, axis=-1, keepdims=True)
        acc_sc[...] = alpha * acc_sc[...] + jnp.dot(p.astype(_BF), k[:, :KV_LORA], preferred_element_type=_F32)
        m_sc[...] = m_new

    @pl.when(kj == pl.num_programs(2) - 1)
    def _():
        o_lat = acc_sc[...] / l_sc[...]
        for h in range(MLA_HEADS):
            oh = o_lat[h * tq:(h + 1) * tq].astype(_BF)
            o_ref[0, :, h * V_DIM:(h + 1) * V_DIM] = jnp.dot(oh, wv_ref[h], preferred_element_type=_F32)


def _mla_prompt_attn(qcat, kcat, w_v, tq=256, tk=512):
    bsz, _, seq, _ = qcat.shape
    tq = min(tq, seq)
    tk = min(tk, seq)
    assert tq & (tq - 1) == 0, "query tile must be a power of two (row -> position uses a bit mask)"

    def k_index(b, i, j):
        return (b, jnp.minimum(j, (i * tq + tq - 1) // tk), 0)

    return pl.pallas_call(
        functools.partial(_mla_attn_kernel, tq=tq, tk=tk),
        grid=(bsz, seq // tq, seq // tk),
        in_specs=[pl.BlockSpec((1, MLA_HEADS, tq, MLA_KW), lambda b, i, j: (b, 0, i, 0)),
                  pl.BlockSpec((1, tk, MLA_KW), k_index),
                  pl.BlockSpec(w_v.shape, lambda b, i, j: (0, 0, 0))],
        out_specs=pl.BlockSpec((1, tq, MLA_HEADS * V_DIM), lambda b, i, j: (b, i, 0)),
        out_shape=jax.ShapeDtypeStruct((bsz, seq, MLA_HEADS * V_DIM), _F32),
        scratch_shapes=[pltpu.VMEM((MLA_HEADS * tq, 1), _F32),
                        pltpu.VMEM((MLA_HEADS * tq, 1), _F32),
                        pltpu.VMEM((MLA_HEADS * tq, KV_LORA), _F32)],
        compiler_params=_params("parallel", "parallel", "arbitrary"),
        name="mla_prompt_attn",
    )(qcat, kcat, w_v)


def _mla_decode_kernel(pt_ref, q_ref, knew_ref, wv_ref, fold_ref, *refs, pages):
    lat_refs = refs[:pages]
    kr_refs = refs[pages:2 * pages]
    o_ref = refs[2 * pages]
    m_sc, l_sc, acc_sc = refs[2 * pages + 1:]
    g = pl.program_id(1)

    @pl.when(g == 0)
    def _():
        m_sc[...] = jnp.full_like(m_sc, _NEG_INF)
        l_sc[...] = jnp.zeros_like(l_sc)
        acc_sc[...] = jnp.zeros_like(acc_sc)

    q = q_ref[0]
    q_abs = q[:, :KV_LORA]
    q_pe = jnp.dot(q[:, KV_LORA:], fold_ref[...], preferred_element_type=_F32).astype(_BF)

    m_run, l_run, acc = m_sc[...], l_sc[...], acc_sc[...]
    for p in range(pages):
        c = lat_refs[p][0, 0].astype(_BF)
        kr = kr_refs[p][0, 0].astype(_BF)
        s = (lax.dot_general(q_abs, c, _NT, preferred_element_type=_F32)
             + lax.dot_general(q_pe, kr, _NT, preferred_element_type=_F32)) * MLA_SCALE
        m_new = jnp.maximum(m_run, jnp.max(s, axis=-1, keepdims=True))
        alpha = jnp.exp(m_run - m_new)
        pr = jnp.exp(s - m_new)
        l_run = alpha * l_run + jnp.sum(pr, axis=-1, keepdims=True)
        acc = alpha * acc + jnp.dot(pr.astype(_BF), c, preferred_element_type=_F32)
        m_run = m_new
    m_sc[...], l_sc[...], acc_sc[...] = m_run, l_run, acc

    @pl.when(g == pl.num_programs(1) - 1)
    def _():
        k_new = knew_ref[0].astype(_F32)
        s_self = jnp.sum(q.astype(_F32) * k_new, axis=-1, keepdims=True) * MLA_SCALE
        m_fin = jnp.maximum(m_run, s_self)
        alpha = jnp.exp(m_run - m_fin)
        p_self = jnp.exp(s_self - m_fin)
        l_fin = alpha * l_run + p_self
        c_new = k_new[:, :KV_LORA]
        o_lat = (alpha * acc + p_self.astype(_BF).astype(_F32) * c_new) / l_fin
        for h in range(MLA_HEADS):
            o_ref[0, :, h * V_DIM:(h + 1) * V_DIM] = jnp.dot(
                o_lat[h:h + 1].astype(_BF), wv_ref[h], preferred_element_type=_F32)


def _mla_decode_attn(qcat, kcat, cache_lat, cache_kr, page_table, w_v, pages=8):
    bsz = qcat.shape[0]
    n_pages = page_table.shape[1]
    lat4 = cache_lat.reshape((1,) + cache_lat.shape)
    kr4 = cache_kr.reshape((1,) + cache_kr.shape)

    def page_spec(width, p):
        return pl.BlockSpec((1, 1, PAGE_SIZE, width), lambda b, g, pt, p=p: (0, pt[b, g * pages + p], 0, 0))

    lane = jnp.arange(2 * LANES)
    fold = (jnp.arange(ROPE_DIM)[None, :] == ((lane // LANES) * ROPE_HALF + lane % ROPE_HALF)[:, None]).astype(_BF)
    in_specs = [pl.BlockSpec((1, MLA_HEADS, MLA_KW), lambda b, g, pt: (b, 0, 0)),
                pl.BlockSpec((1, 1, MLA_KW), lambda b, g, pt: (b, 0, 0)),
                pl.BlockSpec(w_v.shape, lambda b, g, pt: (0, 0, 0)),
                pl.BlockSpec(fold.shape, lambda b, g, pt: (0, 0))]
    in_specs += [page_spec(KV_LORA, p) for p in range(pages)]
    in_specs += [page_spec(ROPE_DIM, p) for p in range(pages)]
    out = pl.pallas_call(
        functools.partial(_mla_decode_kernel, pages=pages),
        grid_spec=pltpu.PrefetchScalarGridSpec(
            num_scalar_prefetch=1,
            grid=(bsz, n_pages // pages),
            in_specs=in_specs,
            out_specs=pl.BlockSpec((1, 1, MLA_HEADS * V_DIM), lambda b, g, pt: (b, 0, 0)),
            scratch_shapes=[pltpu.VMEM((MLA_HEADS, 1), _F32),
                            pltpu.VMEM((MLA_HEADS, 1), _F32),
                            pltpu.VMEM((MLA_HEADS, KV_LORA), _F32)]),
        out_shape=jax.ShapeDtypeStruct((bsz, 1, MLA_HEADS * V_DIM), _F32),
        compiler_params=_params("parallel", "arbitrary"),
        name="mla_decode_attn",
    )(page_table, qcat, kcat, w_v, fold, *([lat4] * pages), *([kr4] * pages))
    return out.reshape(bsz, MLA_HEADS * V_DIM)


_CONV_HALO = 32


def _conv_prompt_kernel(u_ref, wdw_ref, bdw_ref, g_ref, b_ref, o_ref, ext_sc, *, tl):
    @pl.when(pl.program_id(1) == 0)
    def _():
        ext_sc[:_CONV_HALO, :] = jnp.zeros((_CONV_HALO, CONV_DIM), _F32)

    ext_sc[_CONV_HALO:, :] = u_ref[0]
    first = _CONV_HALO - (CONV_WIDTH - 1)
    y = jnp.zeros((tl, CONV_DIM), _F32) + bdw_ref[...]
    for w in range(CONV_WIDTH):
        y = y + ext_sc[first + w:first + w + tl, :] * wdw_ref[w:w + 1, :]
    y = _layer_norm_rows(y, g_ref[...], b_ref[...])
    o_ref[0] = y * jax.nn.sigmoid(y)
    ext_sc[:_CONV_HALO, :] = ext_sc[tl:tl + _CONV_HALO, :]


def _conv_prompt(u, w_dw, b_dw, ln_g, ln_b, tl=256):
    bsz, seq, d = u.shape
    tl = min(tl, seq)
    vec = pl.BlockSpec((1, d), lambda b, l: (0, 0))
    return pl.pallas_call(
        functools.partial(_conv_prompt_kernel, tl=tl),
        grid=(bsz, seq // tl),
        in_specs=[pl.BlockSpec((1, tl, d), lambda b, l: (b, l, 0)),
                  pl.BlockSpec((CONV_WIDTH, d), lambda b, l: (0, 0)), vec, vec, vec],
        out_specs=pl.BlockSpec((1, tl, d), lambda b, l: (b, l, 0)),
        out_shape=jax.ShapeDtypeStruct((bsz, seq, d), _F32),
        scratch_shapes=[pltpu.VMEM((_CONV_HALO + tl, d), _F32)],
        compiler_params=_params("parallel", "arbitrary"),
        name="conv_prompt",
    )(u, w_dw, b_dw.reshape(1, d), ln_g.reshape(1, d), ln_b.reshape(1, d))


def _conv_step_kernel(u_ref, st_ref, wdw_ref, bdw_ref, g_ref, b_ref, o_ref, st_out_ref):
    st = st_ref[0]
    u = u_ref[0]
    hist = CONV_WIDTH - 1
    y = (jnp.sum(st * wdw_ref[:hist, :], axis=0, keepdims=True) + u * wdw_ref[hist:CONV_WIDTH, :]
         + bdw_ref[...])
    y = _layer_norm_rows(y, g_ref[...], b_ref[...])
    o_ref[0] = y * jax.nn.sigmoid(y)
    st_out_ref[0, :hist - 1, :] = st[1:, :]
    st_out_ref[0, hist - 1:, :] = u


def _conv_step(u, state, w_dw, b_dw, ln_g, ln_b):
    bsz, d = u.shape
    hist = CONV_WIDTH - 1
    vec = pl.BlockSpec((1, d), lambda b: (0, 0))
    y, st_new = pl.pallas_call(
        _conv_step_kernel,
        grid=(bsz,),
        in_specs=[pl.BlockSpec((1, 1, d), lambda b: (b, 0, 0)),
                  pl.BlockSpec((1, hist, d), lambda b: (b, 0, 0)),
                  pl.BlockSpec((CONV_WIDTH, d), lambda b: (0, 0)), vec, vec, vec],
        out_specs=[pl.BlockSpec((1, 1, d), lambda b: (b, 0, 0)),
                   pl.BlockSpec((1, hist, d), lambda b: (b, 0, 0))],
        out_shape=[jax.ShapeDtypeStruct((bsz, 1, d), _F32),
                   jax.ShapeDtypeStruct((bsz, hist, d), _F32)],
        compiler_params=_params("parallel"),
        name="conv_step",
    )(u.reshape(bsz, 1, d), state, w_dw, b_dw.reshape(1, d), ln_g.reshape(1, d), ln_b.reshape(1, d))
    return y.reshape(bsz, d), st_new


def _top_values(s, count):
    n = s.shape[0]
    idx = lax.broadcasted_iota(jnp.int32, s.shape, 0)
    vals = []
    for i in range(count):
        m = jnp.max(s, axis=0, keepdims=True)
        vals.append(m)
        if i + 1 < count:
            first = jnp.min(jnp.where(s == m, idx, n), axis=0, keepdims=True)
            s = jnp.where(idx == first, _NEG_INF, s)
    return vals


def _peer_route_kernel(x_ref, wq_ref, keys_ref, s1_ref, s2_ref, ea_ref, eb_ref, tau_ref):
    xb = x_ref[...].astype(_BF)
    q_t = lax.dot_general(wq_ref[...], xb, _NT, preferred_element_type=_F32)
    tm = xb.shape[0]
    pad_rows = -len(_PEER_CELLS) % 8
    for h in range(PEER_HEADS):
        scores, tops = [], []
        for p in range(2):
            r0 = (h * 2 + p) * PEER_HALF
            q_hp = q_t[r0:r0 + PEER_HALF, :].astype(_BF)
            s = jnp.dot(keys_ref[p, h], q_hp, preferred_element_type=_F32)
            scores.append(s)
            tops.append(_top_values(s, PEER_TOPK))
        s1, s2 = scores
        v1, v2 = tops
        cells = [v1[a] + v2[b] for a, b in _PEER_CELLS]
        cand = jnp.concatenate(cells + [jnp.full((pad_rows, tm), _NEG_INF, _F32)], axis=0)
        tau = _top_values(cand, PEER_TOPK)[-1]
        e1 = [jnp.exp(v - v1[0]) for v in v1]
        e2 = [jnp.exp(v - v2[0]) for v in v2]
        z = jnp.zeros((1, tm), _F32)
        for (a, b), c in zip(_PEER_CELLS, cells):
            z = z + jnp.where(c >= tau, e1[a] * e2[b], 0.0)
        s1_ref[h] = s1
        s2_ref[h] = s2
        ea_ref[h] = jnp.exp(s1 - v1[0]) / z
        eb_ref[h] = jnp.exp(s2 - v2[0])
        tau_ref[h:h + 1, :] = tau


def _peer_expert_kernel(x_ref, u_ref, vt_ref, s1_ref, s2_ref, ea_ref, eb_ref, tau_ref, o_ref, acc_sc, *, te):
    e = pl.program_id(1)

    @pl.when(e == 0)
    def _():
        acc_sc[...] = jnp.zeros_like(acc_sc)

    h_t = lax.dot_general(u_ref[...], x_ref[...], _NT, preferred_element_type=_F32)
    act = 0.5 * h_t * (1.0 + lax.erf(h_t * (2.0 ** -0.5)))
    tm = h_t.shape[1]
    parts = []
    for ii in range(te // N_KEYS):
        i = e * (te // N_KEYS) + ii
        w = jnp.zeros((N_KEYS, tm), _F32)
        for h in range(PEER_HEADS):
            s1_row = s1_ref[h, pl.ds(i, 1), :]
            ea_row = ea_ref[h, pl.ds(i, 1), :]
            keep = (s1_row + s2_ref[h]) >= tau_ref[h:h + 1, :]
            w = w + jnp.where(keep, ea_row * eb_ref[h], 0.0)
        parts.append((w * act[ii * N_KEYS:(ii + 1) * N_KEYS, :]).astype(_BF))
    p_t = jnp.concatenate(parts, axis=0)
    acc_sc[...] += jnp.dot(vt_ref[...], p_t, preferred_element_type=_F32)

    @pl.when(e == pl.num_programs(1) - 1)
    def _():
        o_ref[...] = acc_sc[...].T


def _peer(x, wq_t, keys, u_bf, v_t, tm=256, te=1024):
    n, d = x.shape
    tm = min(tm, n)
    rows = PEER_HEADS * 2 * PEER_HALF
    sc_spec = pl.BlockSpec((PEER_HEADS, N_KEYS, tm), lambda t: (0, 0, t))
    sc_shape = jax.ShapeDtypeStruct((PEER_HEADS, N_KEYS, n), _F32)
    s1, s2, ea, eb, tau = pl.pallas_call(
        _peer_route_kernel,
        grid=(n // tm,),
        in_specs=[pl.BlockSpec((tm, d), lambda t: (t, 0)),
                  pl.BlockSpec((rows, d), lambda t: (0, 0)),
                  pl.BlockSpec(keys.shape, lambda t: (0, 0, 0, 0))],
        out_specs=[sc_spec, sc_spec, sc_spec, sc_spec, pl.BlockSpec((PEER_HEADS, tm), lambda t: (0, t))],
        out_shape=[sc_shape, sc_shape, sc_shape, sc_shape, jax.ShapeDtypeStruct((PEER_HEADS, n), _F32)],
        compiler_params=_params("parallel"),
        name="peer_route",
    )(x, wq_t, keys)

    sc2 = pl.BlockSpec((PEER_HEADS, N_KEYS, tm), lambda t, e: (0, 0, t))
    return pl.pallas_call(
        functools.partial(_peer_expert_kernel, te=te),
        grid=(n // tm, N_EXPERTS // te),
        in_specs=[pl.BlockSpec((tm, d), lambda t, e: (t, 0)),
                  pl.BlockSpec((te, d), lambda t, e: (e, 0)),
                  pl.BlockSpec((d, te), lambda t, e: (0, e)),
                  sc2, sc2, sc2, sc2,
                  pl.BlockSpec((PEER_HEADS, tm), lambda t, e: (0, t))],
        out_specs=pl.BlockSpec((tm, d), lambda t, e: (t, 0)),
        out_shape=jax.ShapeDtypeStruct((n, d), _F32),
        scratch_shapes=[pltpu.VMEM((d, tm), _F32)],
        compiler_params=_params("parallel", "arbitrary"),
        name="peer_experts",
    )(x.astype(_BF), u_bf, v_t, s1, s2, ea, eb, tau)


def _rope_tables(pos):
    inv_freq = ROPE_THETA ** (-jnp.arange(ROPE_HALF, dtype=_F32) / ROPE_HALF)
    ang = pos.astype(_F32)[:, None] * inv_freq[None, :]
    return jnp.tile(jnp.cos(ang), (1, MLA_HEADS)), jnp.tile(jnp.sin(ang), (1, MLA_HEADS))


def _mixer_weights(w_in, w_uq, w_ukv, w_out):
    w_hg = w_in[:, :HG_IN].astype(_BF)
    o = HG_IN
    cq_w = w_in[:, o:o + Q_LORA]
    ckv_w = w_in[:, o + Q_LORA:o + Q_LORA + KV_LORA]
    kr_w = w_in[:, o + Q_LORA + KV_LORA:]
    w_mla = jnp.concatenate([cq_w, ckv_w, jnp.tile(kr_w[:, :ROPE_HALF], (1, MLA_HEADS)),
                             jnp.tile(kr_w[:, ROPE_HALF:], (1, MLA_HEADS))], axis=1).astype(_BF)
    wq = w_uq.reshape(Q_LORA, MLA_HEADS, NOPE_DIM + ROPE_DIM)
    w_q = jnp.concatenate([wq[:, :, :NOPE_DIM].reshape(Q_LORA, -1),
                           wq[:, :, NOPE_DIM:NOPE_DIM + ROPE_HALF].reshape(Q_LORA, -1),
                           wq[:, :, NOPE_DIM + ROPE_HALF:].reshape(Q_LORA, -1)], axis=1).astype(_BF)
    w_k = jnp.transpose(w_ukv[:, :, :NOPE_DIM], (1, 2, 0)).astype(_BF)
    w_v = jnp.transpose(w_ukv[:, :, NOPE_DIM:], (1, 0, 2)).astype(_BF)
    return w_hg, w_mla, w_q, w_k, w_v, w_out[:HG_V_W].astype(_BF), w_out[HG_V_W:].astype(_BF)


def kernel(x_prompt, x_sample, cache_mla_latent, cache_mla_krope, state_hgrn, state_conv, page_table, hg_lb_logits, mix_w_in, mla_q_norm, mla_w_uq, mla_kv_norm, mla_w_ukv, hg_out_norm, mix_w_out, conv_w_pw1, conv_b_pw1, conv_w_dw, conv_b_dw, conv_ln_g, conv_ln_b, conv_w_pw2, conv_b_pw2, peer_w_q, peer_sub_keys, peer_u, peer_v, ln_g, ln_b):
    bp, lp, d = x_prompt.shape
    bs, ls, _ = x_sample.shape
    assert ls == 1, "the sample group decodes one token per sequence"
    past_len = page_table.shape[1] * PAGE_SIZE
    lb_all = jnp.cumsum(jax.nn.softmax(hg_lb_logits.astype(_F32), axis=0), axis=0)
    head_mask = (jnp.arange(LANES)[None, :] // ROPE_HALF == jnp.arange(MLA_HEADS)[:, None]).astype(_F32)
    cos_p, sin_p = _rope_tables(jnp.arange(lp))
    cos_s, sin_s = _rope_tables(jnp.full((bs,), past_len))

    hp = x_prompt.reshape(bp * lp, d)
    hs = x_sample.reshape(bs, d)
    outs_p = {"lat": [], "kpe": [], "hg": [], "conv": []}
    outs_s = {"lat": [], "kpe": [], "hg": [], "conv": []}
    for layer in range(DEPTH):
        if layer % 2 == 0:
            e = layer // 2
            lb = lb_all[layer]
            w_hg, w_mla, w_q, w_k, w_v, w_out_hg, w_out_mla = _mixer_weights(
                mix_w_in[e], mla_w_uq[e], mla_w_ukv[e], mix_w_out[e])
            hg = _linear([hp], [w_hg]).reshape(bp, lp, HG_IN)
            o_hg, s_new = _hgrn_prompt(hg, lb, hg_out_norm[e])
            qcat, kcat, c_kv, k_pe = _mla_prep(hp.reshape(bp, lp, d), w_mla, mla_q_norm[e], w_q, mla_kv_norm[e],
                                               w_k, cos_p, sin_p, head_mask)
            o_mla = _mla_prompt_attn(qcat, kcat, w_v)
            mp = _linear([o_hg.reshape(bp * lp, HG_V_W), o_mla.reshape(bp * lp, MLA_HEADS * V_DIM)],
                         [w_out_hg, w_out_mla])
            outs_p["lat"].append(c_kv)
            outs_p["kpe"].append(k_pe)
            outs_p["hg"].append(s_new)
            hg = _linear([hs], [w_hg])
            o_hg, s_new = _hgrn_step(hg, state_hgrn[e], lb, hg_out_norm[e])
            qcat, kcat, c_kv, k_pe = _mla_prep(hs.reshape(1, bs, d), w_mla, mla_q_norm[e], w_q, mla_kv_norm[e],
                                               w_k, cos_s, sin_s, head_mask)
            o_mla = _mla_decode_attn(jnp.transpose(qcat[0], (1, 0, 2)), kcat.reshape(bs, 1, MLA_KW),
                                     cache_mla_latent[e], cache_mla_krope[e], page_table, w_v)
            ms = _linear([o_hg, o_mla], [w_out_hg, w_out_mla])
            outs_s["lat"].append(c_kv.reshape(bs, 1, KV_LORA))
            outs_s["kpe"].append(k_pe.reshape(bs, 1, ROPE_DIM))
            outs_s["hg"].append(s_new)
        else:
            o = layer // 2
            w1 = conv_w_pw1[o].astype(_BF)
            w2 = conv_w_pw2[o].astype(_BF)
            u = _glu_linear(hp, w1, conv_b_pw1[o]).reshape(bp, lp, CONV_DIM)
            y = _conv_prompt(u, conv_w_dw[o], conv_b_dw[o], conv_ln_g[o], conv_ln_b[o])
            mp = _linear([y.reshape(bp * lp, CONV_DIM)], [w2], conv_b_pw2[o])
            outs_p["conv"].append(u[:, lp - (CONV_WIDTH - 1):, :])
            u = _glu_linear(hs, w1, conv_b_pw1[o])
            y, st_new = _conv_step(u, state_conv[o], conv_w_dw[o], conv_b_dw[o], conv_ln_g[o], conv_ln_b[o])
            ms = _linear([y], [w2], conv_b_pw2[o])
            outs_s["conv"].append(st_new)
        hp = _add_ln(hp, mp, ln_g[layer, 0], ln_b[layer, 0])
        hs = _add_ln(hs, ms, ln_g[layer, 0], ln_b[layer, 0])
        wq_t = peer_w_q[layer].T.astype(_BF)
        keys = peer_sub_keys[layer].astype(_BF)
        u_bf = peer_u[layer].astype(_BF)
        v_t = peer_v[layer].T.astype(_BF)
        hp = _add_ln(hp, _peer(hp, wq_t, keys, u_bf, v_t), ln_g[layer, 1], ln_b[layer, 1])
        hs = _add_ln(hs, _peer(hs, wq_t, keys, u_bf, v_t, tm=128), ln_g[layer, 1], ln_b[layer, 1])
    stack = lambda xs: jnp.stack(xs)
    return (hp.reshape(bp, lp, d), hs.reshape(bs, ls, d),
            stack(outs_p["lat"]), stack(outs_p["kpe"]), stack(outs_p["hg"]), stack(outs_p["conv"]),
            stack(outs_s["lat"]), stack(outs_s["kpe"]), stack(outs_s["hg"]), stack(outs_s["conv"]))
```

```python
import functools
import math

import jax
import jax.numpy as jnp
from jax import lax
from jax.experimental import pallas as pl
from jax.experimental.pallas import tpu as pltpu

D_MODEL = 1024
DEPTH = 2
PAGE_SIZE = 128
HG_HEADS = 4
HG_DK = 128
HG_DV = 128
HG_CHUNK = 64
HG_SUB = 16
MLA_HEADS = 4
Q_LORA = 384
KV_LORA = 256
NOPE_DIM = 128
ROPE_DIM = 64
ROPE_HALF = ROPE_DIM // 2
V_DIM = 128
ROPE_THETA = 10000.0
MLA_SCALE = (NOPE_DIM + ROPE_DIM) ** -0.5
CONV_DIM = D_MODEL
CONV_WIDTH = 31
PEER_HEADS = 8
N_KEYS = 128
N_EXPERTS = N_KEYS * N_KEYS
PEER_HALF = 128
PEER_TOPK = 16
DN_ALPHA = (2 * DEPTH) ** 0.25
LN_EPS = 1e-5
RMS_EPS = 1e-6
HG_QK_W = HG_HEADS * HG_DK
HG_V_W = HG_HEADS * HG_DV
HG_IN = 2 * HG_QK_W + 2 * HG_V_W
MLA_KW = KV_LORA + 2 * 128

LANES = 128
SUBLANES = 8
MXU_WIDTH = 256
VMEM_LIMIT = 56 * 1024 * 1024

_BF = jnp.bfloat16
_F32 = jnp.float32
_NT = (((1,), (1,)), ((), ()))
_TN = (((0,), (0,)), ((), ()))
_NEG_INF = float("-inf")


def _params(*sem):
    return pltpu.CompilerParams(dimension_semantics=sem, vmem_limit_bytes=VMEM_LIMIT)


def _linear_kernel(*refs, n_in, has_bias):
    o_ref = refs[-1]
    acc = None
    for x_ref, w_ref in zip(refs[:n_in], refs[n_in:2 * n_in]):
        d = jnp.dot(x_ref[...].astype(_BF), w_ref[...], preferred_element_type=_F32)
        acc = d if acc is None else acc + d
    if has_bias:
        acc = acc + refs[2 * n_in][...]
    o_ref[...] = acc


def _linear(xs, ws, bias=None, tm=512):
    m = xs[0].shape[0]
    n = ws[0].shape[1]
    tm = min(tm, m)
    in_specs = [pl.BlockSpec((tm, x.shape[1]), lambda i: (i, 0)) for x in xs]
    in_specs += [pl.BlockSpec(w.shape, lambda i: (0, 0)) for w in ws]
    args = list(xs) + list(ws)
    if bias is not None:
        in_specs.append(pl.BlockSpec((1, n), lambda i: (0, 0)))
        args.append(bias.reshape(1, n))
    return pl.pallas_call(
        functools.partial(_linear_kernel, n_in=len(xs), has_bias=bias is not None),
        grid=(m // tm,),
        in_specs=in_specs,
        out_specs=pl.BlockSpec((tm, n), lambda i: (i, 0)),
        out_shape=jax.ShapeDtypeStruct((m, n), _F32),
        compiler_params=_params("parallel"),
        name="linear",
    )(*args)


def _glu_kernel(x_ref, wa_ref, wb_ref, ba_ref, bb_ref, o_ref):
    xb = x_ref[...].astype(_BF)
    a = jnp.dot(xb, wa_ref[...], preferred_element_type=_F32) + ba_ref[...]
    g = jnp.dot(xb, wb_ref[...], preferred_element_type=_F32) + bb_ref[...]
    o_ref[...] = a * jax.nn.sigmoid(g)


def _glu_linear(x, w, b, tm=512):
    m, k = x.shape
    n = w.shape[1] // 2
    tm = min(tm, m)
    b2 = b.reshape(1, 2 * n)
    return pl.pallas_call(
        _glu_kernel,
        grid=(m // tm,),
        in_specs=[pl.BlockSpec((tm, k), lambda i: (i, 0)),
                  pl.BlockSpec((k, n), lambda i: (0, 0)),
                  pl.BlockSpec((k, n), lambda i: (0, 1)),
                  pl.BlockSpec((1, n), lambda i: (0, 0)),
                  pl.BlockSpec((1, n), lambda i: (0, 1))],
        out_specs=pl.BlockSpec((tm, n), lambda i: (i, 0)),
        out_shape=jax.ShapeDtypeStruct((m, n), _F32),
        compiler_params=_params("parallel"),
        name="glu_linear",
    )(x, w, w, b2, b2)


def _layer_norm_rows(v, g, b):
    mu = jnp.mean(v, axis=-1, keepdims=True)
    c = v - mu
    var = jnp.mean(c * c, axis=-1, keepdims=True)
    return c * lax.rsqrt(var + LN_EPS) * g + b


def _add_ln_kernel(x_ref, f_ref, g_ref, b_ref, o_ref):
    o_ref[...] = _layer_norm_rows(DN_ALPHA * x_ref[...] + f_ref[...], g_ref[...], b_ref[...])


def _add_ln(x, f, g, b, tm=512):
    m, d = x.shape
    tm = min(tm, m)
    row = pl.BlockSpec((tm, d), lambda i: (i, 0))
    vec = pl.BlockSpec((1, d), lambda i: (0, 0))
    return pl.pallas_call(
        _add_ln_kernel,
        grid=(m // tm,),
        in_specs=[row, row, vec, vec],
        out_specs=row,
        out_shape=jax.ShapeDtypeStruct((m, d), _F32),
        compiler_params=_params("parallel"),
        name="add_ln",
    )(x, f, g.reshape(1, d), b.reshape(1, d))


def _hg_out(o, gate, norm_row):
    y = o * lax.rsqrt(jnp.mean(o * o, axis=-1, keepdims=True) + RMS_EPS) * norm_row
    return y * (gate * jax.nn.sigmoid(gate))


def _hgrn_prompt_kernel(q_ref, f_ref, i_ref, g_ref, lb_ref, norm_ref, o_ref, s_ref, st_sc):
    c = HG_CHUNK
    step = pl.program_id(1)

    @pl.when(step == 0)
    def _():
        st_sc[...] = jnp.zeros_like(st_sc)

    row = lax.broadcasted_iota(jnp.int32, (c, c), 0)
    col = lax.broadcasted_iota(jnp.int32, (c, c), 1)
    tril = (col <= row).astype(_F32)
    sub_row = lax.broadcasted_iota(jnp.int32, (HG_SUB, HG_DK), 0)
    norm_row = norm_ref[...]

    for h in range(HG_HEADS):
        sl = slice(h * HG_DK, (h + 1) * HG_DK)
        lb = lb_ref[:, sl]
        q = q_ref[0, :, sl]
        z = f_ref[0, :, sl]
        v = i_ref[0, :, sl]
        log_f = jnp.log(lb + (1.0 - lb) * jax.nn.sigmoid(z))
        k = (1.0 - lb) * jax.nn.sigmoid(-z)
        b = jnp.dot(tril, log_f, preferred_element_type=_F32, precision=lax.Precision.HIGHEST)
        b_last = b[c - 1:c, :]
        st = st_sc[h]
        v_bf = v.astype(_BF)
        o = lax.dot_general((q * jnp.exp(b)).astype(_BF), st.astype(_BF), _NT, preferred_element_type=_F32)
        pieces = []
        for blk in range(c // HG_SUB):
            r0 = blk * HG_SUB
            rs = slice(r0, r0 + HG_SUB)
            qb, kb, vb, bb = q[rs], k[rs], v[rs], b[rs]
            ob = o[rs]
            if blk > 0:
                ref_row = b[r0 - 1:r0, :]
                qd = (qb * jnp.exp(bb - ref_row)).astype(_BF)
                kd = (k[:r0] * jnp.exp(ref_row - b[:r0])).astype(_BF)
                a_off = lax.dot_general(qd, kd, _NT, preferred_element_type=_F32)
                ob = ob + jnp.dot(a_off.astype(_BF), v_bf[:r0], preferred_element_type=_F32)
            for s in range(HG_SUB):
                w = jnp.where(sub_row >= s, jnp.exp(bb - bb[s:s + 1, :]), 0.0)
                a_col = jnp.sum(qb * kb[s:s + 1, :] * w, axis=-1, keepdims=True)
                ob = ob + a_col * vb[s:s + 1, :]
            pieces.append(ob)
        o_full = jnp.concatenate(pieces, axis=0)
        o_ref[0, :, sl] = _hg_out(o_full, g_ref[0, :, sl], norm_row)
        kd_all = (k * jnp.exp(b_last - b)).astype(_BF)
        st_new = st * jnp.exp(b_last) + jnp.dot(v.T.astype(_BF), kd_all, preferred_element_type=_F32)
        st_sc[h] = st_new

        @pl.when(step == pl.num_programs(1) - 1)
        def _():
            s_ref[0, h] = st_new.T


def _hgrn_prompt(hg, lb, out_norm):
    bsz, seq, _ = hg.shape
    c = HG_CHUNK
    w = HG_QK_W
    blk = lambda j: pl.BlockSpec((1, c, w), lambda b, l, j=j: (b, l, j))
    return pl.pallas_call(
        _hgrn_prompt_kernel,
        grid=(bsz, seq // c),
        in_specs=[blk(0), blk(1), blk(2), blk(3),
                  pl.BlockSpec((1, w), lambda b, l: (0, 0)),
                  pl.BlockSpec((1, HG_DV), lambda b, l: (0, 0))],
        out_specs=[pl.BlockSpec((1, c, HG_V_W), lambda b, l: (b, l, 0)),
                   pl.BlockSpec((1, HG_HEADS, HG_DK, HG_DV), lambda b, l: (b, 0, 0, 0))],
        out_shape=[jax.ShapeDtypeStruct((bsz, seq, HG_V_W), _F32),
                   jax.ShapeDtypeStruct((bsz, HG_HEADS, HG_DK, HG_DV), _F32)],
        scratch_shapes=[pltpu.VMEM((HG_HEADS, HG_DV, HG_DK), _F32)],
        compiler_params=_params("parallel", "arbitrary"),
        name="hgrn_prompt",
    )(hg, hg, hg, hg, lb.reshape(1, w), out_norm.reshape(1, HG_DV))


def _hgrn_step_kernel(cols_ref, row_ref, lbc_ref, norm_ref, s_ref, o_ref, s_out_ref):
    norm_row = norm_ref[...]
    for h in range(HG_HEADS):
        sl_v = slice(2 * HG_QK_W + h * HG_DV, 2 * HG_QK_W + (h + 1) * HG_DV)
        sl_g = slice(2 * HG_QK_W + HG_V_W + h * HG_DV, 2 * HG_QK_W + HG_V_W + (h + 1) * HG_DV)
        qc = cols_ref[0, h, :, 0:1]
        zc = cols_ref[0, h, :, 1:2]
        lbc = lbc_ref[h]
        f = lbc + (1.0 - lbc) * jax.nn.sigmoid(zc)
        kc = (1.0 - lbc) * jax.nn.sigmoid(-zc)
        v_row = row_ref[0, :, sl_v]
        s_new = f * s_ref[0, h] + kc * v_row
        s_out_ref[0, h] = s_new
        o = jnp.sum(qc * s_new, axis=0, keepdims=True)
        o_ref[0, :, h * HG_DV:(h + 1) * HG_DV] = _hg_out(o, row_ref[0, :, sl_g], norm_row)


def _hgrn_step(hg, state, lb, out_norm):
    bsz = hg.shape[0]
    cols = hg[:, :2 * HG_QK_W].reshape(bsz, 2, HG_HEADS, HG_DK).transpose(0, 2, 3, 1)
    rows = hg.reshape(bsz, 1, HG_IN)
    lbc = lb.reshape(HG_HEADS, HG_DK, 1)
    st_spec = pl.BlockSpec((1, HG_HEADS, HG_DK, HG_DV), lambda b: (b, 0, 0, 0))
    o, s_new = pl.pallas_call(
        _hgrn_step_kernel,
        grid=(bsz,),
        in_specs=[pl.BlockSpec((1, HG_HEADS, HG_DK, 2), lambda b: (b, 0, 0, 0)),
                  pl.BlockSpec((1, 1, HG_IN), lambda b: (b, 0, 0)),
                  pl.BlockSpec((HG_HEADS, HG_DK, 1), lambda b: (0, 0, 0)),
                  pl.BlockSpec((1, HG_DV), lambda b: (0, 0)),
                  st_spec],
        out_specs=[pl.BlockSpec((1, 1, HG_V_W), lambda b: (b, 0, 0)), st_spec],
        out_shape=[jax.ShapeDtypeStruct((bsz, 1, HG_V_W), _F32),
                   jax.ShapeDtypeStruct(state.shape, _F32)],
        compiler_params=_params("parallel"),
        name="hgrn_step",
    )(cols, rows, lbc, out_norm.reshape(1, HG_DV), state)
    return o.reshape(bsz, HG_V_W), s_new


def _rms_rows(x, g):
    return x * lax.rsqrt(jnp.mean(x * x, axis=-1, keepdims=True) + RMS_EPS) * g


def _mla_prep_kernel(x_ref, win_ref, qn_ref, wq_ref, kvn_ref, wk_ref, cos_ref, sin_ref, hm_ref,
                     qcat_ref, kcat_ref, ckv_ref, kpe_ref):
    m = jnp.dot(x_ref[0].astype(_BF), win_ref[...], preferred_element_type=_F32)
    cq = _rms_rows(m[:, :Q_LORA], qn_ref[...])
    c_kv = _rms_rows(m[:, Q_LORA:Q_LORA + KV_LORA], kvn_ref[...])
    kr1 = m[:, Q_LORA + KV_LORA:Q_LORA + KV_LORA + LANES]
    kr2 = m[:, Q_LORA + KV_LORA + LANES:]
    cos = cos_ref[...]
    sin = sin_ref[...]
    k1 = kr1 * cos - kr2 * sin
    k2 = kr1 * sin + kr2 * cos
    ckv_ref[0] = c_kv
    lane = lax.broadcasted_iota(jnp.int32, k1.shape, 1)
    kpe_ref[0] = jnp.where(lane < ROPE_HALF, k1, k2)[:, :ROPE_DIM]
    kcat_ref[0, :, :KV_LORA] = c_kv.astype(_BF)
    kcat_ref[0, :, KV_LORA:KV_LORA + LANES] = k1.astype(_BF)
    kcat_ref[0, :, KV_LORA + LANES:] = k2.astype(_BF)

    qh = jnp.dot(cq.astype(_BF), wq_ref[...], preferred_element_type=_F32)
    nope_w = MLA_HEADS * NOPE_DIM
    q1 = qh[:, nope_w:nope_w + LANES]
    q2 = qh[:, nope_w + LANES:]
    r1 = q1 * cos - q2 * sin
    r2 = q1 * sin + q2 * cos
    for h in range(MLA_HEADS):
        q_nope = qh[:, h * NOPE_DIM:(h + 1) * NOPE_DIM].astype(_BF)
        q_abs = jnp.dot(q_nope, wk_ref[h], preferred_element_type=_F32)
        hm = hm_ref[h:h + 1, :]
        qcat_ref[0, h, :, :KV_LORA] = q_abs.astype(_BF)
        qcat_ref[0, h, :, KV_LORA:KV_LORA + LANES] = (r1 * hm).astype(_BF)
        qcat_ref[0, h, :, KV_LORA + LANES:] = (r2 * hm).astype(_BF)


def _mla_prep(x, w_in, q_norm, w_q, kv_norm, w_k, cos, sin, head_mask, tm=256):
    bsz, seq, d = x.shape
    tm = min(tm, seq)
    full = lambda a: pl.BlockSpec(a.shape, lambda b, l: (0,) * a.ndim)
    qn = q_norm.reshape(1, Q_LORA)
    kvn = kv_norm.reshape(1, KV_LORA)
    return pl.pallas_call(
        _mla_prep_kernel,
        grid=(bsz, seq // tm),
        in_specs=[pl.BlockSpec((1, tm, d), lambda b, l: (b, l, 0)),
                  full(w_in), full(qn), full(w_q), full(kvn), full(w_k),
                  pl.BlockSpec((tm, LANES), lambda b, l: (l, 0)),
                  pl.BlockSpec((tm, LANES), lambda b, l: (l, 0)),
                  full(head_mask)],
        out_specs=[pl.BlockSpec((1, MLA_HEADS, tm, MLA_KW), lambda b, l: (b, 0, l, 0)),
                   pl.BlockSpec((1, tm, MLA_KW), lambda b, l: (b, l, 0)),
                   pl.BlockSpec((1, tm, KV_LORA), lambda b, l: (b, l, 0)),
                   pl.BlockSpec((1, tm, ROPE_DIM), lambda b, l: (b, l, 0))],
        out_shape=[jax.ShapeDtypeStruct((bsz, MLA_HEADS, seq, MLA_KW), _BF),
                   jax.ShapeDtypeStruct((bsz, seq, MLA_KW), _BF),
                   jax.ShapeDtypeStruct((bsz, seq, KV_LORA), _F32),
                   jax.ShapeDtypeStruct((bsz, seq, ROPE_DIM), _F32)],
        compiler_params=_params("parallel", "parallel"),
        name="mla_prep",
    )(x, w_in, qn, w_q, kvn, w_k, cos, sin, head_mask)


def _mla_attn_kernel(q_ref, k_ref, wv_ref, o_ref, m_sc, l_sc, acc_sc, *, tq, tk):
    qi = pl.program_id(1)
    kj = pl.program_id(2)
    rows = MLA_HEADS * tq

    @pl.when(kj == 0)
    def _():
        m_sc[...] = jnp.full_like(m_sc, _NEG_INF)
        l_sc[...] = jnp.zeros_like(l_sc)
        acc_sc[...] = jnp.zeros_like(acc_sc)

    @pl.when(kj * tk <= qi * tq + (tq - 1))
    def _():
        q = q_ref[0].reshape(rows, MLA_KW)
        k = k_ref[0]
        s = lax.dot_general(q, k, _NT, preferred_element_type=_F32) * MLA_SCALE
        q_pos = qi * tq + (lax.broadcasted_iota(jnp.int32, (rows, tk), 0) & (tq - 1))
        k_pos = kj * tk + lax.broadcasted_iota(jnp.int32, (rows, tk), 1)
        s = jnp.where(k_pos <= q_pos, s, _NEG_INF)
        m_old = m_sc[...]
        m_new = jnp.maximum(m_old, jnp.max(s, axis=-1, keepdims=True))
        alpha = jnp.exp(m_old - m_new)
        p = jnp.exp(s - m_new)
        l_sc[...] = alpha * l_sc[...] + jnp.sum(p, axis=-1, keepdims=True)
        acc_sc[...] = alpha * acc_sc[...] + jnp.dot(p.astype(_BF), k[:, :KV_LORA], preferred_element_type=_F32)
        m_sc[...] = m_new

    @pl.when(kj == pl.num_programs(2) - 1)
    def _():
        o_lat = acc_sc[...] / l_sc[...]
        for h in range(MLA_HEADS):
            oh = o_lat[h * tq:(h + 1) * tq].astype(_BF)
            o_ref[0, :, h * V_DIM:(h + 1) * V_DIM] = jnp.dot(oh, wv_ref[h], preferred_element_type=_F32)


def _mla_prompt_attn(qcat, kcat, w_v, tq=256, tk=512):
    bsz, _, seq, _ = qcat.shape
    tq = min(tq, seq)
    tk = min(tk, seq)
    assert tq & (tq - 1) == 0, "query tile must be a power of two (row -> position uses a bit mask)"

    def k_index(b, i, j):
        return (b, jnp.minimum(j, (i * tq + tq - 1) // tk), 0)

    return pl.pallas_call(
        functools.partial(_mla_attn_kernel, tq=tq, tk=tk),
        grid=(bsz, seq // tq, seq // tk),
        in_specs=[pl.BlockSpec((1, MLA_HEADS, tq, MLA_KW), lambda b, i, j: (b, 0, i, 0)),
                  pl.BlockSpec((1, tk, MLA_KW), k_index),
                  pl.BlockSpec(w_v.shape, lambda b, i, j: (0, 0, 0))],
        out_specs=pl.BlockSpec((1, tq, MLA_HEADS * V_DIM), lambda b, i, j: (b, i, 0)),
        out_shape=jax.ShapeDtypeStruct((bsz, seq, MLA_HEADS * V_DIM), _F32),
        scratch_shapes=[pltpu.VMEM((MLA_HEADS * tq, 1), _F32),
                        pltpu.VMEM((MLA_HEADS * tq, 1), _F32),
                        pltpu.VMEM((MLA_HEADS * tq, KV_LORA), _F32)],
        compiler_params=_params("parallel", "parallel", "arbitrary"),
        name="mla_prompt_attn",
    )(qcat, kcat, w_v)


def _mla_decode_kernel(pt_ref, q_ref, knew_ref, wv_ref, fold_ref, *refs, pages):
    lat_refs = refs[:pages]
    kr_refs = refs[pages:2 * pages]
    o_ref = refs[2 * pages]
    m_sc, l_sc, acc_sc = refs[2 * pages + 1:]
    g = pl.program_id(1)

    @pl.when(g == 0)
    def _():
        m_sc[...] = jnp.full_like(m_sc, _NEG_INF)
        l_sc[...] = jnp.zeros_like(l_sc)
        acc_sc[...] = jnp.zeros_like(acc_sc)

    q = q_ref[0]
    q_abs = q[:, :KV_LORA]
    q_pe = jnp.dot(q[:, KV_LORA:], fold_ref[...], preferred_element_type=_F32).astype(_BF)

    lat = [lat_refs[p][0, 0].astype(_BF) for p in range(pages)]
    s = jnp.concatenate(
        [lax.dot_general(q_abs, lat[p], _NT, preferred_element_type=_F32)
         + lax.dot_general(q_pe, kr_refs[p][0, 0].astype(_BF), _NT, preferred_element_type=_F32)
         for p in range(pages)], axis=1) * MLA_SCALE
    m_old = m_sc[...]
    m_run = jnp.maximum(m_old, jnp.max(s, axis=-1, keepdims=True))
    alpha = jnp.exp(m_old - m_run)
    pr = jnp.exp(s - m_run)
    l_run = alpha * l_sc[...] + jnp.sum(pr, axis=-1, keepdims=True)
    pr = pr.astype(_BF)
    pv = None
    for p in range(pages):
        d = jnp.dot(pr[:, p * PAGE_SIZE:(p + 1) * PAGE_SIZE], lat[p], preferred_element_type=_F32)
        pv = d if pv is None else pv + d
    acc = alpha * acc_sc[...] + pv
    m_sc[...], l_sc[...], acc_sc[...] = m_run, l_run, acc

    @pl.when(g == pl.num_programs(1) - 1)
    def _():
        k_new = knew_ref[0].astype(_F32)
        s_self = jnp.sum(q.astype(_F32) * k_new, axis=-1, keepdims=True) * MLA_SCALE
        m_fin = jnp.maximum(m_run, s_self)
        alpha = jnp.exp(m_run - m_fin)
        p_self = jnp.exp(s_self - m_fin)
        l_fin = alpha * l_run + p_self
        c_new = k_new[:, :KV_LORA]
        o_lat = (alpha * acc + p_self.astype(_BF).astype(_F32) * c_new) / l_fin
        for h in range(MLA_HEADS):
            o_ref[0, :, h * V_DIM:(h + 1) * V_DIM] = jnp.dot(
                o_lat[h:h + 1].astype(_BF), wv_ref[h], preferred_element_type=_F32)


def _mla_decode_attn(qcat, kcat, cache_lat, cache_kr, page_table, w_v, pages=16):
    bsz = qcat.shape[0]
    n_pages = page_table.shape[1]
    lat4 = cache_lat.reshape((1,) + cache_lat.shape)
    kr4 = cache_kr.reshape((1,) + cache_kr.shape)

    def page_spec(width, p):
        return pl.BlockSpec((1, 1, PAGE_SIZE, width), lambda b, g, pt, p=p: (0, pt[b, g * pages + p], 0, 0))

    lane = jnp.arange(2 * LANES)
    fold = (jnp.arange(ROPE_DIM)[None, :] == ((lane // LANES) * ROPE_HALF + lane % ROPE_HALF)[:, None]).astype(_BF)
    in_specs = [pl.BlockSpec((1, MLA_HEADS, MLA_KW), lambda b, g, pt: (b, 0, 0)),
                pl.BlockSpec((1, 1, MLA_KW), lambda b, g, pt: (b, 0, 0)),
                pl.BlockSpec(w_v.shape, lambda b, g, pt: (0, 0, 0)),
                pl.BlockSpec(fold.shape, lambda b, g, pt: (0, 0))]
    in_specs += [page_spec(KV_LORA, p) for p in range(pages)]
    in_specs += [page_spec(ROPE_DIM, p) for p in range(pages)]
    out = pl.pallas_call(
        functools.partial(_mla_decode_kernel, pages=pages),
        grid_spec=pltpu.PrefetchScalarGridSpec(
            num_scalar_prefetch=1,
            grid=(bsz, n_pages // pages),
            in_specs=in_specs,
            out_specs=pl.BlockSpec((1, 1, MLA_HEADS * V_DIM), lambda b, g, pt: (b, 0, 0)),
            scratch_shapes=[pltpu.VMEM((MLA_HEADS, 1), _F32),
                            pltpu.VMEM((MLA_HEADS, 1), _F32),
                            pltpu.VMEM((MLA_HEADS, KV_LORA), _F32)]),
        out_shape=jax.ShapeDtypeStruct((bsz, 1, MLA_HEADS * V_DIM), _F32),
        compiler_params=_params("parallel", "arbitrary"),
        name="mla_decode_attn",
    )(page_table, qcat, kcat, w_v, fold, *([lat4] * pages), *([kr4] * pages))
    return out.reshape(bsz, MLA_HEADS * V_DIM)


_CONV_HALO = 32


def _conv_prompt_kernel(u_ref, wdw_ref, bdw_ref, g_ref, b_ref, o_ref, ext_sc, *, tl):
    @pl.when(pl.program_id(1) == 0)
    def _():
        ext_sc[:_CONV_HALO, :] = jnp.zeros((_CONV_HALO, CONV_DIM), _F32)

    ext_sc[_CONV_HALO:, :] = u_ref[0]
    first = _CONV_HALO - (CONV_WIDTH - 1)
    y = jnp.zeros((tl, CONV_DIM), _F32) + bdw_ref[...]
    for w in range(CONV_WIDTH):
        y = y + ext_sc[first + w:first + w + tl, :] * wdw_ref[w:w + 1, :]
    y = _layer_norm_rows(y, g_ref[...], b_ref[...])
    o_ref[0] = y * jax.nn.sigmoid(y)
    ext_sc[:_CONV_HALO, :] = ext_sc[tl:tl + _CONV_HALO, :]


def _conv_prompt(u, w_dw, b_dw, ln_g, ln_b, tl=256):
    bsz, seq, d = u.shape
    tl = min(tl, seq)
    vec = pl.BlockSpec((1, d), lambda b, l: (0, 0))
    return pl.pallas_call(
        functools.partial(_conv_prompt_kernel, tl=tl),
        grid=(bsz, seq // tl),
        in_specs=[pl.BlockSpec((1, tl, d), lambda b, l: (b, l, 0)),
                  pl.BlockSpec((CONV_WIDTH, d), lambda b, l: (0, 0)), vec, vec, vec],
        out_specs=pl.BlockSpec((1, tl, d), lambda b, l: (b, l, 0)),
        out_shape=jax.ShapeDtypeStruct((bsz, seq, d), _F32),
        scratch_shapes=[pltpu.VMEM((_CONV_HALO + tl, d), _F32)],
        compiler_params=_params("parallel", "arbitrary"),
        name="conv_prompt",
    )(u, w_dw, b_dw.reshape(1, d), ln_g.reshape(1, d), ln_b.reshape(1, d))


def _conv_step_kernel(u_ref, st_ref, wdw_ref, bdw_ref, g_ref, b_ref, o_ref, st_out_ref):
    st = st_ref[0]
    u = u_ref[0]
    hist = CONV_WIDTH - 1
    y = (jnp.sum(st * wdw_ref[:hist, :], axis=0, keepdims=True) + u * wdw_ref[hist:CONV_WIDTH, :]
         + bdw_ref[...])
    y = _layer_norm_rows(y, g_ref[...], b_ref[...])
    o_ref[0] = y * jax.nn.sigmoid(y)
    st_out_ref[0, :hist - 1, :] = st[1:, :]
    st_out_ref[0, hist - 1:, :] = u


def _conv_step(u, state, w_dw, b_dw, ln_g, ln_b):
    bsz, d = u.shape
    hist = CONV_WIDTH - 1
    vec = pl.BlockSpec((1, d), lambda b: (0, 0))
    y, st_new = pl.pallas_call(
        _conv_step_kernel,
        grid=(bsz,),
        in_specs=[pl.BlockSpec((1, 1, d), lambda b: (b, 0, 0)),
                  pl.BlockSpec((1, hist, d), lambda b: (b, 0, 0)),
                  pl.BlockSpec((CONV_WIDTH, d), lambda b: (0, 0)), vec, vec, vec],
        out_specs=[pl.BlockSpec((1, 1, d), lambda b: (b, 0, 0)),
                   pl.BlockSpec((1, hist, d), lambda b: (b, 0, 0))],
        out_shape=[jax.ShapeDtypeStruct((bsz, 1, d), _F32),
                   jax.ShapeDtypeStruct((bsz, hist, d), _F32)],
        compiler_params=_params("parallel"),
        name="conv_step",
    )(u.reshape(bsz, 1, d), state, w_dw, b_dw.reshape(1, d), ln_g.reshape(1, d), ln_b.reshape(1, d))
    return y.reshape(bsz, d), st_new


def _oddeven_merge(lo, hi, r):
    step = r * 2
    if step < hi - lo:
        yield from _oddeven_merge(lo, hi, step)
        yield from _oddeven_merge(lo + r, hi, step)
        yield from [(i, i + r) for i in range(lo + r, hi - r, step)]
    else:
        yield (lo, lo + r)


def _oddeven_sort(lo, hi):
    if hi - lo >= 1:
        mid = lo + (hi - lo) // 2
        yield from _oddeven_sort(lo, mid)
        yield from _oddeven_sort(mid + 1, hi)
        yield from _oddeven_merge(lo, hi, 1)


_SORT_NET = tuple(_oddeven_sort(0, PEER_TOPK - 1))
_MERGE_NET = tuple((i, i + d) for d in (8, 4, 2, 1) for i in range(PEER_TOPK) if i & d == 0)


def _exchange(x, net):
    for i, j in net:
        x[i], x[j] = jnp.maximum(x[i], x[j]), jnp.minimum(x[i], x[j])


def _merge_sublanes(x):
    for shift in (4, 2, 1):
        x = [jnp.maximum(x[i], pltpu.roll(x[PEER_TOPK - 1 - i], shift, 0)) for i in range(PEER_TOPK)]
        _exchange(x, _MERGE_NET)
    return x


def _top16(s):
    x = [s[SUBLANES * v:SUBLANES * (v + 1), :] for v in range(PEER_TOPK)]
    _exchange(x, _SORT_NET)
    return _merge_sublanes(x)


def _peer_route_kernel(x_ref, wq_ref, keys_ref, s1_ref, s2_ref, ea_ref, eb_ref, tau_ref):
    xb = x_ref[...].astype(_BF)
    q_t = lax.dot_general(wq_ref[...], xb, _NT, preferred_element_type=_F32)
    tm = xb.shape[0]
    sub = lax.broadcasted_iota(jnp.int32, (SUBLANES, tm), 0)
    for h in range(PEER_HEADS):
        scores, tops = [], []
        for p in range(2):
            r0 = (h * 2 + p) * PEER_HALF
            q_hp = q_t[r0:r0 + PEER_HALF, :].astype(_BF)
            s = jnp.dot(keys_ref[p, h], q_hp, preferred_element_type=_F32)
            scores.append(s)
            tops.append(_top16(s))
        s1, s2 = scores
        y1, y2 = tops
        v1_lo, v1_hi = y1[0], y1[SUBLANES]
        for a in range(1, SUBLANES):
            v1_lo = jnp.where(sub == a, y1[a], v1_lo)
            v1_hi = jnp.where(sub == a, y1[SUBLANES + a], v1_hi)
        cells = []
        for b in range(PEER_TOPK):
            c = v1_lo + y2[b]
            lim = PEER_TOPK // (b + 1)
            cells.append(c if lim >= SUBLANES else jnp.where(sub < lim, c, _NEG_INF))
        extra = v1_hi + y2[0]
        x = list(cells)
        x[PEER_TOPK - 1] = jnp.maximum(x[PEER_TOPK - 1], extra)
        _exchange(x, _MERGE_NET)
        tau = _merge_sublanes(x)[PEER_TOPK - 1]
        m1, m2 = y1[0], y2[0]
        e1_lo = jnp.exp(v1_lo - m1)
        z = jnp.where(extra >= tau, jnp.exp(v1_hi - m1), 0.0)
        for b in range(PEER_TOPK):
            z = z + jnp.where(cells[b] >= tau, e1_lo * jnp.exp(y2[b] - m2), 0.0)
        z = jnp.sum(z, axis=0, keepdims=True)
        ea = jnp.exp(s1 - m1[0:1, :]) / z
        eb = jnp.exp(s2 - m2[0:1, :])
        for cb in range(tm // LANES):
            cols = slice(cb * LANES, (cb + 1) * LANES)
            s1_ref[h, cb] = s1[:, cols]
            s2_ref[h, cb] = s2[:, cols]
            ea_ref[h, cb] = ea[:, cols]
            eb_ref[h, cb] = eb[:, cols]
        tau_ref[h:h + 1, :] = tau[0:1, :]


def _peer_expert_kernel(xt_ref, u_ref, vt_ref, s1_ref, s2_ref, ea_ref, eb_ref, tau_ref, o_ref,
                        h0_sc, h1_sc, p0_sc, p1_sc, acc_sc, *, te, ne):
    g = pl.program_id(0)
    e_c = jnp.maximum(g - 2, 0) % ne

    @pl.when(g == 0)
    def _():
        for ref in (h0_sc, h1_sc, p0_sc, p1_sc):
            ref[...] = jnp.zeros_like(ref)

    @pl.when((g < 2) | (e_c == 0))
    def _():
        acc_sc[...] = jnp.zeros_like(acc_sc)

    def gate_tile(h_b, p_b, ii, cb):
        rows = slice(ii * N_KEYS, (ii + 1) * N_KEYS)
        cols = slice(cb * LANES, (cb + 1) * LANES)
        groups = N_KEYS // SUBLANES
        hh = h_b[cb, rows, :]
        act = 0.5 * hh * (1.0 + lax.erf(hh * (2.0 ** -0.5)))
        w = jnp.zeros((groups, SUBLANES, LANES), _F32)
        for h in range(PEER_HEADS):
            row = lambda ref: jnp.broadcast_to(ref[h, cb, ii:ii + 1, :], (SUBLANES, LANES))[None]
            tau = jnp.broadcast_to(tau_ref[h:h + 1, cols], (SUBLANES, LANES))[None]
            s2 = s2_ref[h, cb].reshape(groups, SUBLANES, LANES)
            eb = eb_ref[h, cb].reshape(groups, SUBLANES, LANES)
            w = w + jnp.where((row(s1_ref) + s2) >= tau, row(ea_ref) * eb, 0.0)
        p_b[cb, rows, :] = (w.reshape(N_KEYS, LANES) * act).astype(_BF)

    def step(h_a, h_b, p_b, p_c):
        n_cb = h_b.shape[0]
        per = min(MXU_WIDTH // LANES, n_cb)
        n_ii = te // N_KEYS
        d_rows = acc_sc.shape[0] // n_ii
        for ii in range(n_ii):
            rows = slice(ii * N_KEYS, (ii + 1) * N_KEYS)
            out_rows = slice(ii * d_rows, (ii + 1) * d_rows)
            for grp in range(n_cb // per):
                cbs = range(grp * per, (grp + 1) * per)
                cols = slice(grp * per * LANES, (grp + 1) * per * LANES)
                p_grp = jnp.concatenate([p_c[cb] for cb in cbs], axis=1)
                acc_sc[out_rows, cols] += jnp.dot(vt_ref[out_rows, :], p_grp, preferred_element_type=_F32)
                hv = jnp.dot(u_ref[rows, :], xt_ref[:, cols], preferred_element_type=_F32)
                for k, cb in enumerate(cbs):
                    h_a[cb, rows, :] = hv[:, k * LANES:(k + 1) * LANES]
                for cb in cbs:
                    gate_tile(h_b, p_b, ii, cb)

    @pl.when(g % 2 == 0)
    def _():
        step(h0_sc, h1_sc, p1_sc, p0_sc)

    @pl.when(g % 2 == 1)
    def _():
        step(h1_sc, h0_sc, p0_sc, p1_sc)

    @pl.when((g >= 2) & (e_c == ne - 1))
    def _():
        o_ref[...] = acc_sc[...].T


def _peer(x, wq_t, keys, u_bf, v_t, tm=512, te=1024, tr=256):
    n, d = x.shape
    tm = min(tm, n)
    tr = min(tr, n)
    rows = PEER_HEADS * 2 * PEER_HALF
    sc_spec = pl.BlockSpec((PEER_HEADS, tr // LANES, N_KEYS, LANES), lambda t: (0, t, 0, 0))
    sc_shape = jax.ShapeDtypeStruct((PEER_HEADS, n // LANES, N_KEYS, LANES), _F32)
    s1, s2, ea, eb, tau = pl.pallas_call(
        _peer_route_kernel,
        grid=(n // tr,),
        in_specs=[pl.BlockSpec((tr, d), lambda t: (t, 0)),
                  pl.BlockSpec((rows, d), lambda t: (0, 0)),
                  pl.BlockSpec(keys.shape, lambda t: (0, 0, 0, 0))],
        out_specs=[sc_spec, sc_spec, sc_spec, sc_spec, pl.BlockSpec((PEER_HEADS, tr), lambda t: (0, t))],
        out_shape=[sc_shape, sc_shape, sc_shape, sc_shape, jax.ShapeDtypeStruct((PEER_HEADS, n), _F32)],
        compiler_params=_params("parallel"),
        name="peer_route",
    )(x, wq_t, keys)

    nt = n // tm
    ne = N_EXPERTS // te
    last = nt * ne - 1
    step_a = lambda g: jnp.minimum(g, last)
    tile_b = lambda g: jnp.minimum(jnp.maximum(g - 1, 0) // ne, nt - 1)
    step_c = lambda g: jnp.maximum(g - 2, 0)
    assert (te // N_KEYS) % SUBLANES == 0, "first-half key rows of an expert tile must fill whole sublane tiles"
    n_cb = tm // LANES
    sc_all = pl.BlockSpec((PEER_HEADS, n_cb, N_KEYS, LANES), lambda g: (0, tile_b(g), 0, 0))
    sc_rows = pl.BlockSpec((PEER_HEADS, n_cb, te // N_KEYS, LANES),
                           lambda g: (0, tile_b(g), jnp.maximum(g - 1, 0) % ne, 0))
    return pl.pallas_call(
        functools.partial(_peer_expert_kernel, te=te, ne=ne),
        grid=(nt * ne + 2,),
        in_specs=[pl.BlockSpec((d, tm), lambda g: (0, step_a(g) // ne)),
                  pl.BlockSpec((te, d), lambda g: (step_a(g) % ne, 0)),
                  pl.BlockSpec((d, te), lambda g: (0, step_c(g) % ne)),
                  sc_rows, sc_all, sc_rows, sc_all,
                  pl.BlockSpec((PEER_HEADS, tm), lambda g: (0, tile_b(g)))],
        out_specs=pl.BlockSpec((tm, d), lambda g: (step_c(g) // ne, 0)),
        out_shape=jax.ShapeDtypeStruct((n, d), _F32),
        scratch_shapes=[pltpu.VMEM((n_cb, te, LANES), _F32), pltpu.VMEM((n_cb, te, LANES), _F32),
                        pltpu.VMEM((n_cb, te, LANES), _BF), pltpu.VMEM((n_cb, te, LANES), _BF),
                        pltpu.VMEM((d, tm), _F32)],
        compiler_params=_params("arbitrary"),
        name="peer_experts",
    )(x.astype(_BF).T, u_bf, v_t, s1, s2, ea, eb, tau)


def _rope_tables(pos):
    inv_freq = ROPE_THETA ** (-jnp.arange(ROPE_HALF, dtype=_F32) / ROPE_HALF)
    ang = pos.astype(_F32)[:, None] * inv_freq[None, :]
    return jnp.tile(jnp.cos(ang), (1, MLA_HEADS)), jnp.tile(jnp.sin(ang), (1, MLA_HEADS))


def _mixer_weights(w_in, w_uq, w_ukv, w_out):
    w_hg = w_in[:, :HG_IN].astype(_BF)
    o = HG_IN
    cq_w = w_in[:, o:o + Q_LORA]
    ckv_w = w_in[:, o + Q_LORA:o + Q_LORA + KV_LORA]
    kr_w = w_in[:, o + Q_LORA + KV_LORA:]
    w_mla = jnp.concatenate([cq_w, ckv_w, jnp.tile(kr_w[:, :ROPE_HALF], (1, MLA_HEADS)),
                             jnp.tile(kr_w[:, ROPE_HALF:], (1, MLA_HEADS))], axis=1).astype(_BF)
    wq = w_uq.reshape(Q_LORA, MLA_HEADS, NOPE_DIM + ROPE_DIM)
    w_q = jnp.concatenate([wq[:, :, :NOPE_DIM].reshape(Q_LORA, -1),
                           wq[:, :, NOPE_DIM:NOPE_DIM + ROPE_HALF].reshape(Q_LORA, -1),
                           wq[:, :, NOPE_DIM + ROPE_HALF:].reshape(Q_LORA, -1)], axis=1).astype(_BF)
    w_k = jnp.transpose(w_ukv[:, :, :NOPE_DIM], (1, 2, 0)).astype(_BF)
    w_v = jnp.transpose(w_ukv[:, :, NOPE_DIM:], (1, 0, 2)).astype(_BF)
    return w_hg, w_mla, w_q, w_k, w_v, w_out[:HG_V_W].astype(_BF), w_out[HG_V_W:].astype(_BF)


def kernel(x_prompt, x_sample, cache_mla_latent, cache_mla_krope, state_hgrn, state_conv, page_table, hg_lb_logits, mix_w_in, mla_q_norm, mla_w_uq, mla_kv_norm, mla_w_ukv, hg_out_norm, mix_w_out, conv_w_pw1, conv_b_pw1, conv_w_dw, conv_b_dw, conv_ln_g, conv_ln_b, conv_w_pw2, conv_b_pw2, peer_w_q, peer_sub_keys, peer_u, peer_v, ln_g, ln_b):
    bp, lp, d = x_prompt.shape
    bs, ls, _ = x_sample.shape
    assert ls == 1, "the sample group decodes one token per sequence"
    past_len = page_table.shape[1] * PAGE_SIZE
    lb_all = jnp.cumsum(jax.nn.softmax(hg_lb_logits.astype(_F32), axis=0), axis=0)
    head_mask = (jnp.arange(LANES)[None, :] // ROPE_HALF == jnp.arange(MLA_HEADS)[:, None]).astype(_F32)
    cos_p, sin_p = _rope_tables(jnp.arange(lp))
    cos_s, sin_s = _rope_tables(jnp.full((bs,), past_len))

    hp = x_prompt.reshape(bp * lp, d)
    hs = x_sample.reshape(bs, d)
    outs_p = {"lat": [], "kpe": [], "hg": [], "conv": []}
    outs_s = {"lat": [], "kpe": [], "hg": [], "conv": []}
    for layer in range(DEPTH):
        if layer % 2 == 0:
            e = layer // 2
            lb = lb_all[layer]
            w_hg, w_mla, w_q, w_k, w_v, w_out_hg, w_out_mla = _mixer_weights(
                mix_w_in[e], mla_w_uq[e], mla_w_ukv[e], mix_w_out[e])
            hg = _linear([hp], [w_hg]).reshape(bp, lp, HG_IN)
            o_hg, s_new = _hgrn_prompt(hg, lb, hg_out_norm[e])
            qcat, kcat, c_kv, k_pe = _mla_prep(hp.reshape(bp, lp, d), w_mla, mla_q_norm[e], w_q, mla_kv_norm[e],
                                               w_k, cos_p, sin_p, head_mask)
            o_mla = _mla_prompt_attn(qcat, kcat, w_v)
            mp = _linear([o_hg.reshape(bp * lp, HG_V_W), o_mla.reshape(bp * lp, MLA_HEADS * V_DIM)],
                         [w_out_hg, w_out_mla])
            outs_p["lat"].append(c_kv)
            outs_p["kpe"].append(k_pe)
            outs_p["hg"].append(s_new)
            hg = _linear([hs], [w_hg])
            o_hg, s_new = _hgrn_step(hg, state_hgrn[e], lb, hg_out_norm[e])
            qcat, kcat, c_kv, k_pe = _mla_prep(hs.reshape(1, bs, d), w_mla, mla_q_norm[e], w_q, mla_kv_norm[e],
                                               w_k, cos_s, sin_s, head_mask)
            o_mla = _mla_decode_attn(jnp.transpose(qcat[0], (1, 0, 2)), kcat.reshape(bs, 1, MLA_KW),
                                     cache_mla_latent[e], cache_mla_krope[e], page_table, w_v)
            ms = _linear([o_hg, o_mla], [w_out_hg, w_out_mla])
            outs_s["lat"].append(c_kv.reshape(bs, 1, KV_LORA))
            outs_s["kpe"].append(k_pe.reshape(bs, 1, ROPE_DIM))
            outs_s["hg"].append(s_new)
        else:
            o = layer // 2
            w1 = conv_w_pw1[o].astype(_BF)
            w2 = conv_w_pw2[o].astype(_BF)
            u = _glu_linear(hp, w1, conv_b_pw1[o]).reshape(bp, lp, CONV_DIM)
            y = _conv_prompt(u, conv_w_dw[o], conv_b_dw[o], conv_ln_g[o], conv_ln_b[o])
            mp = _linear([y.reshape(bp * lp, CONV_DIM)], [w2], conv_b_pw2[o])
            outs_p["conv"].append(u[:, lp - (CONV_WIDTH - 1):, :])
            u = _glu_linear(hs, w1, conv_b_pw1[o])
            y, st_new = _conv_step(u, state_conv[o], conv_w_dw[o], conv_b_dw[o], conv_ln_g[o], conv_ln_b[o])
            ms = _linear([y], [w2], conv_b_pw2[o])
            outs_s["conv"].append(st_new)
        hp = _add_ln(hp, mp, ln_g[layer, 0], ln_b[layer, 0])
        hs = _add_ln(hs, ms, ln_g[layer, 0], ln_b[layer, 0])
        wq_t = peer_w_q[layer].T.astype(_BF)
        keys = peer_sub_keys[layer].astype(_BF)
        u_bf = peer_u[layer].astype(_BF)
        v_t = peer_v[layer].T.astype(_BF)
        hp = _add_ln(hp, _peer(hp, wq_t, keys, u_bf, v_t), ln_g[layer, 1], ln_b[layer, 1])
        hs = _add_ln(hs, _peer(hs, wq_t, keys, u_bf, v_t), ln_g[layer, 1], ln_b[layer, 1])
    stack = lambda xs: jnp.stack(xs)
    return (hp.reshape(bp, lp, d), hs.reshape(bs, ls, d),
            stack(outs_p["lat"]), stack(outs_p["kpe"]), stack(outs_p["hg"]), stack(outs_p["conv"]),
            stack(outs_s["lat"]), stack(outs_s["kpe"]), stack(outs_s["hg"]), stack(outs_s["conv"]))
```

```python
import functools
import itertools
import math

import jax
import jax.numpy as jnp
from jax import lax
from jax.experimental import pallas as pl
from jax.experimental.pallas import tpu as pltpu

D_MODEL = 1024
DEPTH = 2
PAGE_SIZE = 128
HG_HEADS = 4
HG_DK = 128
HG_DV = 128
HG_CHUNK = 64
HG_SUB = 8
MLA_HEADS = 4
Q_LORA = 384
KV_LORA = 256
NOPE_DIM = 128
ROPE_DIM = 64
ROPE_HALF = ROPE_DIM // 2
V_DIM = 128
ROPE_THETA = 10000.0
MLA_SCALE = (NOPE_DIM + ROPE_DIM) ** -0.5
CONV_DIM = D_MODEL
CONV_WIDTH = 31
PEER_HEADS = 8
N_KEYS = 128
N_EXPERTS = N_KEYS * N_KEYS
PEER_HALF = 128
PEER_TOPK = 16
PEER_TE = 1024
GATE_KEYS = 2
DN_ALPHA = (2 * DEPTH) ** 0.25
LN_EPS = 1e-5
RMS_EPS = 1e-6
HG_QK_W = HG_HEADS * HG_DK
HG_V_W = HG_HEADS * HG_DV
HG_IN = 2 * HG_QK_W + 2 * HG_V_W
MLA_KW = KV_LORA + 2 * 128

LANES = 128
SUBLANES = 8
MXU_WIDTH = 256
VMEM_LIMIT = 56 * 1024 * 1024

_BF = jnp.bfloat16
_F32 = jnp.float32
_NT = (((1,), (1,)), ((), ()))
_TN = (((0,), (0,)), ((), ()))
_NEG_INF = float("-inf")


def _params(*sem):
    return pltpu.CompilerParams(dimension_semantics=sem, vmem_limit_bytes=VMEM_LIMIT)


def _linear_kernel(*refs, n_in, has_bias):
    o_ref = refs[-1]
    acc = None
    for x_ref, w_ref in zip(refs[:n_in], refs[n_in:2 * n_in]):
        d = jnp.dot(x_ref[...].astype(_BF), w_ref[...], preferred_element_type=_F32)
        acc = d if acc is None else acc + d
    if has_bias:
        acc = acc + refs[2 * n_in][...]
    o_ref[...] = acc


def _linear(xs, ws, bias=None, tm=512):
    m = xs[0].shape[0]
    n = ws[0].shape[1]
    tm = min(tm, m)
    in_specs = [pl.BlockSpec((tm, x.shape[1]), lambda i: (i, 0)) for x in xs]
    in_specs += [pl.BlockSpec(w.shape, lambda i: (0, 0)) for w in ws]
    args = list(xs) + list(ws)
    if bias is not None:
        in_specs.append(pl.BlockSpec((1, n), lambda i: (0, 0)))
        args.append(bias.reshape(1, n))
    return pl.pallas_call(
        functools.partial(_linear_kernel, n_in=len(xs), has_bias=bias is not None),
        grid=(m // tm,),
        in_specs=in_specs,
        out_specs=pl.BlockSpec((tm, n), lambda i: (i, 0)),
        out_shape=jax.ShapeDtypeStruct((m, n), _F32),
        compiler_params=_params("parallel"),
        name="linear",
    )(*args)


def _glu_kernel(x_ref, wa_ref, wb_ref, ba_ref, bb_ref, o_ref):
    xb = x_ref[...].astype(_BF)
    a = jnp.dot(xb, wa_ref[...], preferred_element_type=_F32) + ba_ref[...]
    g = jnp.dot(xb, wb_ref[...], preferred_element_type=_F32) + bb_ref[...]
    o_ref[...] = a * jax.nn.sigmoid(g)


def _glu_linear(x, w, b, tm=512):
    m, k = x.shape
    n = w.shape[1] // 2
    tm = min(tm, m)
    b2 = b.reshape(1, 2 * n)
    return pl.pallas_call(
        _glu_kernel,
        grid=(m // tm,),
        in_specs=[pl.BlockSpec((tm, k), lambda i: (i, 0)),
                  pl.BlockSpec((k, n), lambda i: (0, 0)),
                  pl.BlockSpec((k, n), lambda i: (0, 1)),
                  pl.BlockSpec((1, n), lambda i: (0, 0)),
                  pl.BlockSpec((1, n), lambda i: (0, 1))],
        out_specs=pl.BlockSpec((tm, n), lambda i: (i, 0)),
        out_shape=jax.ShapeDtypeStruct((m, n), _F32),
        compiler_params=_params("parallel"),
        name="glu_linear",
    )(x, w, w, b2, b2)


def _layer_norm_rows(v, g, b):
    mu = jnp.mean(v, axis=-1, keepdims=True)
    c = v - mu
    var = jnp.mean(c * c, axis=-1, keepdims=True)
    return c * lax.rsqrt(var + LN_EPS) * g + b


def _add_ln_kernel(x_ref, f_ref, g_ref, b_ref, o_ref):
    o_ref[...] = _layer_norm_rows(DN_ALPHA * x_ref[...] + f_ref[...], g_ref[...], b_ref[...])


def _add_ln(x, f, g, b, tm=512):
    m, d = x.shape
    tm = min(tm, m)
    row = pl.BlockSpec((tm, d), lambda i: (i, 0))
    vec = pl.BlockSpec((1, d), lambda i: (0, 0))
    return pl.pallas_call(
        _add_ln_kernel,
        grid=(m // tm,),
        in_specs=[row, row, vec, vec],
        out_specs=row,
        out_shape=jax.ShapeDtypeStruct((m, d), _F32),
        compiler_params=_params("parallel"),
        name="add_ln",
    )(x, f, g.reshape(1, d), b.reshape(1, d))


def _hg_out(o, gate, norm_row):
    y = o * lax.rsqrt(jnp.mean(o * o, axis=-1, keepdims=True) + RMS_EPS) * norm_row
    return y * (gate * jax.nn.sigmoid(gate))


def _hgrn_prompt_kernel(q_ref, f_ref, i_ref, g_ref, lb_ref, norm_ref, o_ref, s_ref, st_sc):
    c = HG_CHUNK
    step = pl.program_id(1)

    @pl.when(step == 0)
    def _():
        st_sc[...] = jnp.zeros_like(st_sc)

    row = lax.broadcasted_iota(jnp.int32, (c, c), 0)
    col = lax.broadcasted_iota(jnp.int32, (c, c), 1)
    tril = (col <= row).astype(_F32)
    sub_row = lax.broadcasted_iota(jnp.int32, (HG_SUB, HG_DK), 0)
    norm_row = norm_ref[...]

    heads = range(HG_HEADS)
    sls = [slice(h * HG_DK, (h + 1) * HG_DK) for h in heads]
    lb = [lb_ref[:, sl] for sl in sls]
    q = [q_ref[0, :, sl] for sl in sls]
    z = [f_ref[0, :, sl] for sl in sls]
    v = [i_ref[0, :, sl] for sl in sls]
    log_f = [jnp.log(lb[h] + (1.0 - lb[h]) * jax.nn.sigmoid(z[h])) for h in heads]
    k = [(1.0 - lb[h]) * jax.nn.sigmoid(-z[h]) for h in heads]
    b = [jnp.dot(tril, log_f[h], preferred_element_type=_F32, precision=lax.Precision.HIGHEST) for h in heads]
    b_last = [b[h][c - 1:c, :] for h in heads]
    st = [st_sc[h] for h in heads]
    v_bf = [v[h].astype(_BF) for h in heads]
    o = [lax.dot_general((q[h] * jnp.exp(b[h])).astype(_BF), st[h].astype(_BF), _NT, preferred_element_type=_F32)
         for h in heads]
    blocks = range(c // HG_SUB)
    rows = [slice(blk * HG_SUB, (blk + 1) * HG_SUB) for blk in blocks]
    a_off = {}
    for blk in blocks[1:]:
        r0 = blk * HG_SUB
        for h in heads:
            ref_row = b[h][r0 - 1:r0, :]
            qd = (q[h][rows[blk]] * jnp.exp(b[h][rows[blk]] - ref_row)).astype(_BF)
            kd = (k[h][:r0] * jnp.exp(ref_row - b[h][:r0])).astype(_BF)
            a_off[blk, h] = lax.dot_general(qd, kd, _NT, preferred_element_type=_F32)
    a_col = {}
    for blk in blocks:
        for s in range(HG_SUB):
            for h in heads:
                bb = b[h][rows[blk]]
                w = jnp.where(sub_row >= s, jnp.exp(bb - bb[s:s + 1, :]), 0.0)
                k_row = k[h][blk * HG_SUB + s:blk * HG_SUB + s + 1, :]
                a_col[blk, s, h] = jnp.sum(q[h][rows[blk]] * k_row * w, axis=-1, keepdims=True)
    off = {(blk, h): jnp.dot(a_off[blk, h].astype(_BF), v_bf[h][:blk * HG_SUB], preferred_element_type=_F32)
           for blk in blocks[1:] for h in heads}
    for h in heads:
        pieces = []
        for blk in blocks:
            ob = o[h][rows[blk]]
            if blk > 0:
                ob = ob + off[blk, h]
            for s in range(HG_SUB):
                ob = ob + a_col[blk, s, h] * v[h][blk * HG_SUB + s:blk * HG_SUB + s + 1, :]
            pieces.append(ob)
        o_ref[0, :, sls[h]] = _hg_out(jnp.concatenate(pieces, axis=0), g_ref[0, :, sls[h]], norm_row)
    st_new = []
    for h in heads:
        kd_all = (k[h] * jnp.exp(b_last[h] - b[h])).astype(_BF)
        st_new.append(st[h] * jnp.exp(b_last[h])
                      + jnp.dot(v[h].T.astype(_BF), kd_all, preferred_element_type=_F32))
        st_sc[h] = st_new[h]

    @pl.when(step == pl.num_programs(1) - 1)
    def _():
        for h in heads:
            s_ref[0, h] = st_new[h].T


def _hgrn_prompt(hg, lb, out_norm):
    bsz, seq, _ = hg.shape
    c = HG_CHUNK
    w = HG_QK_W
    blk = lambda j: pl.BlockSpec((1, c, w), lambda b, l, j=j: (b, l, j))
    return pl.pallas_call(
        _hgrn_prompt_kernel,
        grid=(bsz, seq // c),
        in_specs=[blk(0), blk(1), blk(2), blk(3),
                  pl.BlockSpec((1, w), lambda b, l: (0, 0)),
                  pl.BlockSpec((1, HG_DV), lambda b, l: (0, 0))],
        out_specs=[pl.BlockSpec((1, c, HG_V_W), lambda b, l: (b, l, 0)),
                   pl.BlockSpec((1, HG_HEADS, HG_DK, HG_DV), lambda b, l: (b, 0, 0, 0))],
        out_shape=[jax.ShapeDtypeStruct((bsz, seq, HG_V_W), _F32),
                   jax.ShapeDtypeStruct((bsz, HG_HEADS, HG_DK, HG_DV), _F32)],
        scratch_shapes=[pltpu.VMEM((HG_HEADS, HG_DV, HG_DK), _F32)],
        compiler_params=_params("parallel", "arbitrary"),
        name="hgrn_prompt",
    )(hg, hg, hg, hg, lb.reshape(1, w), out_norm.reshape(1, HG_DV))


def _hgrn_step_kernel(cols_ref, row_ref, lbc_ref, norm_ref, s_ref, o_ref, s_out_ref):
    norm_row = norm_ref[...]
    for h in range(HG_HEADS):
        sl_v = slice(2 * HG_QK_W + h * HG_DV, 2 * HG_QK_W + (h + 1) * HG_DV)
        sl_g = slice(2 * HG_QK_W + HG_V_W + h * HG_DV, 2 * HG_QK_W + HG_V_W + (h + 1) * HG_DV)
        qc = cols_ref[0, h, :, 0:1]
        zc = cols_ref[0, h, :, 1:2]
        lbc = lbc_ref[h]
        f = lbc + (1.0 - lbc) * jax.nn.sigmoid(zc)
        kc = (1.0 - lbc) * jax.nn.sigmoid(-zc)
        v_row = row_ref[0, :, sl_v]
        s_new = f * s_ref[0, h] + kc * v_row
        s_out_ref[0, h] = s_new
        o = jnp.sum(qc * s_new, axis=0, keepdims=True)
        o_ref[0, :, h * HG_DV:(h + 1) * HG_DV] = _hg_out(o, row_ref[0, :, sl_g], norm_row)


def _hgrn_step(hg, state, lb, out_norm):
    bsz = hg.shape[0]
    cols = hg[:, :2 * HG_QK_W].reshape(bsz, 2, HG_HEADS, HG_DK).transpose(0, 2, 3, 1)
    rows = hg.reshape(bsz, 1, HG_IN)
    lbc = lb.reshape(HG_HEADS, HG_DK, 1)
    st_spec = pl.BlockSpec((1, HG_HEADS, HG_DK, HG_DV), lambda b: (b, 0, 0, 0))
    o, s_new = pl.pallas_call(
        _hgrn_step_kernel,
        grid=(bsz,),
        in_specs=[pl.BlockSpec((1, HG_HEADS, HG_DK, 2), lambda b: (b, 0, 0, 0)),
                  pl.BlockSpec((1, 1, HG_IN), lambda b: (b, 0, 0)),
                  pl.BlockSpec((HG_HEADS, HG_DK, 1), lambda b: (0, 0, 0)),
                  pl.BlockSpec((1, HG_DV), lambda b: (0, 0)),
                  st_spec],
        out_specs=[pl.BlockSpec((1, 1, HG_V_W), lambda b: (b, 0, 0)), st_spec],
        out_shape=[jax.ShapeDtypeStruct((bsz, 1, HG_V_W), _F32),
                   jax.ShapeDtypeStruct(state.shape, _F32)],
        compiler_params=_params("parallel"),
        name="hgrn_step",
    )(cols, rows, lbc, out_norm.reshape(1, HG_DV), state)
    return o.reshape(bsz, HG_V_W), s_new


def _rms_rows(x, g):
    return x * lax.rsqrt(jnp.mean(x * x, axis=-1, keepdims=True) + RMS_EPS) * g


def _mla_prep_kernel(x_ref, win_ref, qn_ref, wq_ref, kvn_ref, wk_ref, cos_ref, sin_ref, hm_ref,
                     qcat_ref, kcat_ref, ckv_ref, kpe_ref):
    m = jnp.dot(x_ref[0].astype(_BF), win_ref[...], preferred_element_type=_F32)
    cq = _rms_rows(m[:, :Q_LORA], qn_ref[...])
    c_kv = _rms_rows(m[:, Q_LORA:Q_LORA + KV_LORA], kvn_ref[...])
    kr1 = m[:, Q_LORA + KV_LORA:Q_LORA + KV_LORA + LANES]
    kr2 = m[:, Q_LORA + KV_LORA + LANES:]
    cos = cos_ref[...]
    sin = sin_ref[...]
    k1 = kr1 * cos - kr2 * sin
    k2 = kr1 * sin + kr2 * cos
    ckv_ref[0] = c_kv
    lane = lax.broadcasted_iota(jnp.int32, k1.shape, 1)
    kpe_ref[0] = jnp.where(lane < ROPE_HALF, k1, k2)[:, :ROPE_DIM]
    kcat_ref[0, :, :KV_LORA] = c_kv.astype(_BF)
    kcat_ref[0, :, KV_LORA:KV_LORA + LANES] = k1.astype(_BF)
    kcat_ref[0, :, KV_LORA + LANES:] = k2.astype(_BF)

    qh = jnp.dot(cq.astype(_BF), wq_ref[...], preferred_element_type=_F32)
    nope_w = MLA_HEADS * NOPE_DIM
    q1 = qh[:, nope_w:nope_w + LANES]
    q2 = qh[:, nope_w + LANES:]
    r1 = q1 * cos - q2 * sin
    r2 = q1 * sin + q2 * cos
    for h in range(MLA_HEADS):
        q_nope = qh[:, h * NOPE_DIM:(h + 1) * NOPE_DIM].astype(_BF)
        q_abs = jnp.dot(q_nope, wk_ref[h], preferred_element_type=_F32)
        hm = hm_ref[h:h + 1, :]
        qcat_ref[0, h, :, :KV_LORA] = q_abs.astype(_BF)
        qcat_ref[0, h, :, KV_LORA:KV_LORA + LANES] = (r1 * hm).astype(_BF)
        qcat_ref[0, h, :, KV_LORA + LANES:] = (r2 * hm).astype(_BF)


def _mla_prep(x, w_in, q_norm, w_q, kv_norm, w_k, cos, sin, head_mask, tm=256):
    bsz, seq, d = x.shape
    tm = min(tm, seq)
    full = lambda a: pl.BlockSpec(a.shape, lambda b, l: (0,) * a.ndim)
    qn = q_norm.reshape(1, Q_LORA)
    kvn = kv_norm.reshape(1, KV_LORA)
    return pl.pallas_call(
        _mla_prep_kernel,
        grid=(bsz, seq // tm),
        in_specs=[pl.BlockSpec((1, tm, d), lambda b, l: (b, l, 0)),
                  full(w_in), full(qn), full(w_q), full(kvn), full(w_k),
                  pl.BlockSpec((tm, LANES), lambda b, l: (l, 0)),
                  pl.BlockSpec((tm, LANES), lambda b, l: (l, 0)),
                  full(head_mask)],
        out_specs=[pl.BlockSpec((1, MLA_HEADS, tm, MLA_KW), lambda b, l: (b, 0, l, 0)),
                   pl.BlockSpec((1, tm, MLA_KW), lambda b, l: (b, l, 0)),
                   pl.BlockSpec((1, tm, KV_LORA), lambda b, l: (b, l, 0)),
                   pl.BlockSpec((1, tm, ROPE_DIM), lambda b, l: (b, l, 0))],
        out_shape=[jax.ShapeDtypeStruct((bsz, MLA_HEADS, seq, MLA_KW), _BF),
                   jax.ShapeDtypeStruct((bsz, seq, MLA_KW), _BF),
                   jax.ShapeDtypeStruct((bsz, seq, KV_LORA), _F32),
                   jax.ShapeDtypeStruct((bsz, seq, ROPE_DIM), _F32)],
        compiler_params=_params("parallel", "parallel"),
        name="mla_prep",
    )(x, w_in, qn, w_q, kvn, w_k, cos, sin, head_mask)


def _mla_attn_kernel(q_ref, k_ref, wv_ref, o_ref, m_sc, l_sc, acc_sc, *, tq, tk):
    qi = pl.program_id(1)
    kj = pl.program_id(2)
    rows = MLA_HEADS * tq

    @pl.when(kj == 0)
    def _():
        m_sc[...] = jnp.full_like(m_sc, _NEG_INF)
        l_sc[...] = jnp.zeros_like(l_sc)
        acc_sc[...] = jnp.zeros_like(acc_sc)

    @pl.when(kj * tk <= qi * tq + (tq - 1))
    def _():
        q = q_ref[0].reshape(rows, MLA_KW)
        k = k_ref[0]
        s = lax.dot_general(q, k, _NT, preferred_element_type=_F32) * MLA_SCALE
        q_pos = qi * tq + (lax.broadcasted_iota(jnp.int32, (rows, tk), 0) & (tq - 1))
        k_pos = kj * tk + lax.broadcasted_iota(jnp.int32, (rows, tk), 1)
        s = jnp.where(k_pos <= q_pos, s, _NEG_INF)
        m_old = m_sc[...]
        m_new = jnp.maximum(m_old, jnp.max(s, axis=-1, keepdims=True))
        alpha = jnp.exp(m_old - m_new)
        p = jnp.exp(s - m_new)
        l_sc[...] = alpha * l_sc[...] + jnp.sum(p, axis=-1, keepdims=True)
        acc_sc[...] = alpha * acc_sc[...] + jnp.dot(p.astype(_BF), k[:, :KV_LORA], preferred_element_type=_F32)
        m_sc[...] = m_new

    @pl.when(kj == pl.num_programs(2) - 1)
    def _():
        o_lat = acc_sc[...] / l_sc[...]
        for h in range(MLA_HEADS):
            oh = o_lat[h * tq:(h + 1) * tq].astype(_BF)
            o_ref[0, :, h * V_DIM:(h + 1) * V_DIM] = jnp.dot(oh, wv_ref[h], preferred_element_type=_F32)


def _mla_prompt_attn(qcat, kcat, w_v, tq=256, tk=512):
    bsz, _, seq, _ = qcat.shape
    tq = min(tq, seq)
    tk = min(tk, seq)
    assert tq & (tq - 1) == 0, "query tile must be a power of two (row -> position uses a bit mask)"

    def k_index(b, i, j):
        return (b, jnp.minimum(j, (i * tq + tq - 1) // tk), 0)

    return pl.pallas_call(
        functools.partial(_mla_attn_kernel, tq=tq, tk=tk),
        grid=(bsz, seq // tq, seq // tk),
        in_specs=[pl.BlockSpec((1, MLA_HEADS, tq, MLA_KW), lambda b, i, j: (b, 0, i, 0)),
                  pl.BlockSpec((1, tk, MLA_KW), k_index),
                  pl.BlockSpec(w_v.shape, lambda b, i, j: (0, 0, 0))],
        out_specs=pl.BlockSpec((1, tq, MLA_HEADS * V_DIM), lambda b, i, j: (b, i, 0)),
        out_shape=jax.ShapeDtypeStruct((bsz, seq, MLA_HEADS * V_DIM), _F32),
        scratch_shapes=[pltpu.VMEM((MLA_HEADS * tq, 1), _F32),
                        pltpu.VMEM((MLA_HEADS * tq, 1), _F32),
                        pltpu.VMEM((MLA_HEADS * tq, KV_LORA), _F32)],
        compiler_params=_params("parallel", "parallel", "arbitrary"),
        name="mla_prompt_attn",
    )(qcat, kcat, w_v)


def _mla_decode_kernel(pt_ref, q_ref, knew_ref, wv_ref, fold_ref, *refs, pages):
    lat_refs = refs[:pages]
    kr_refs = refs[pages:2 * pages]
    o_ref = refs[2 * pages]
    m_sc, l_sc, acc_sc = refs[2 * pages + 1:]
    g = pl.program_id(1)

    @pl.when(g == 0)
    def _():
        m_sc[...] = jnp.full_like(m_sc, _NEG_INF)
        l_sc[...] = jnp.zeros_like(l_sc)
        acc_sc[...] = jnp.zeros_like(acc_sc)

    q = q_ref[0]
    q_abs = q[:, :KV_LORA]
    q_pe = jnp.dot(q[:, KV_LORA:], fold_ref[...], preferred_element_type=_F32).astype(_BF)

    lat = [lat_refs[p][0, 0].astype(_BF) for p in range(pages)]
    s = jnp.concatenate(
        [lax.dot_general(q_abs, lat[p], _NT, preferred_element_type=_F32)
         + jnp.dot(q_pe, kr_refs[p][0].astype(_BF), preferred_element_type=_F32)
         for p in range(pages)], axis=1) * MLA_SCALE
    m_old = m_sc[...]
    m_run = jnp.maximum(m_old, jnp.max(s, axis=-1, keepdims=True))
    alpha = jnp.exp(m_old - m_run)
    pr = jnp.exp(s - m_run)
    l_run = alpha * l_sc[...] + jnp.sum(pr, axis=-1, keepdims=True)
    pr = pr.astype(_BF)
    pv = functools.reduce(jnp.add, [jnp.dot(pr[:, p * PAGE_SIZE:(p + 1) * PAGE_SIZE], lat[p],
                                            preferred_element_type=_F32) for p in range(pages)])
    acc = alpha * acc_sc[...] + pv
    m_sc[...], l_sc[...], acc_sc[...] = m_run, l_run, acc

    @pl.when(g == pl.num_programs(1) - 1)
    def _():
        k_new = knew_ref[0].astype(_F32)
        s_self = jnp.sum(q.astype(_F32) * k_new, axis=-1, keepdims=True) * MLA_SCALE
        m_fin = jnp.maximum(m_run, s_self)
        alpha = jnp.exp(m_run - m_fin)
        p_self = jnp.exp(s_self - m_fin)
        l_fin = alpha * l_run + p_self
        c_new = k_new[:, :KV_LORA]
        o_lat = (alpha * acc + p_self.astype(_BF).astype(_F32) * c_new) / l_fin
        for h in range(MLA_HEADS):
            o_ref[0, :, h * V_DIM:(h + 1) * V_DIM] = jnp.dot(
                o_lat[h:h + 1].astype(_BF), wv_ref[h], preferred_element_type=_F32)


def _mla_decode_attn(qcat, kcat, cache_lat, cache_kr, page_table, w_v, pages=16):
    bsz = qcat.shape[0]
    n_pages = page_table.shape[1]
    lat4 = cache_lat.reshape((1,) + cache_lat.shape)
    kr3 = jnp.swapaxes(cache_kr, 1, 2)

    def lat_spec(p):
        return pl.BlockSpec((1, 1, PAGE_SIZE, KV_LORA), lambda b, g, pt: (0, pt[b, g * pages + p], 0, 0))

    def kr_spec(p):
        return pl.BlockSpec((1, ROPE_DIM, PAGE_SIZE), lambda b, g, pt: (pt[b, g * pages + p], 0, 0))

    lane = jnp.arange(2 * LANES)
    dst = (lane // LANES) * ROPE_HALF + lane % ROPE_HALF
    fold = (jnp.arange(ROPE_DIM)[None, :] == dst[:, None]).astype(_BF)
    in_specs = [pl.BlockSpec((1, MLA_HEADS, MLA_KW), lambda b, g, pt: (b, 0, 0)),
                pl.BlockSpec((1, 1, MLA_KW), lambda b, g, pt: (b, 0, 0)),
                pl.BlockSpec(w_v.shape, lambda b, g, pt: (0, 0, 0)),
                pl.BlockSpec(fold.shape, lambda b, g, pt: (0, 0))]
    in_specs += [lat_spec(p) for p in range(pages)]
    in_specs += [kr_spec(p) for p in range(pages)]
    out = pl.pallas_call(
        functools.partial(_mla_decode_kernel, pages=pages),
        grid_spec=pltpu.PrefetchScalarGridSpec(
            num_scalar_prefetch=1,
            grid=(bsz, n_pages // pages),
            in_specs=in_specs,
            out_specs=pl.BlockSpec((1, 1, MLA_HEADS * V_DIM), lambda b, g, pt: (b, 0, 0)),
            scratch_shapes=[pltpu.VMEM((MLA_HEADS, 1), _F32),
                            pltpu.VMEM((MLA_HEADS, 1), _F32),
                            pltpu.VMEM((MLA_HEADS, KV_LORA), _F32)]),
        out_shape=jax.ShapeDtypeStruct((bsz, 1, MLA_HEADS * V_DIM), _F32),
        compiler_params=_params("parallel", "arbitrary"),
        name="mla_decode_attn",
    )(page_table, qcat, kcat, w_v, fold, *([lat4] * pages), *([kr3] * pages))
    return out.reshape(bsz, MLA_HEADS * V_DIM)


_CONV_HALO = 32


def _conv_prompt_kernel(u_ref, wdw_ref, bdw_ref, g_ref, b_ref, o_ref, ext_sc, *, tl):
    @pl.when(pl.program_id(1) == 0)
    def _():
        ext_sc[:_CONV_HALO, :] = jnp.zeros((_CONV_HALO, CONV_DIM), _F32)

    ext_sc[_CONV_HALO:, :] = u_ref[0]
    first = _CONV_HALO - (CONV_WIDTH - 1)
    y = jnp.zeros((tl, CONV_DIM), _F32) + bdw_ref[...]
    for w in range(CONV_WIDTH):
        y = y + ext_sc[first + w:first + w + tl, :] * wdw_ref[w:w + 1, :]
    y = _layer_norm_rows(y, g_ref[...], b_ref[...])
    o_ref[0] = y * jax.nn.sigmoid(y)
    ext_sc[:_CONV_HALO, :] = ext_sc[tl:tl + _CONV_HALO, :]


def _conv_prompt(u, w_dw, b_dw, ln_g, ln_b, tl=256):
    bsz, seq, d = u.shape
    tl = min(tl, seq)
    vec = pl.BlockSpec((1, d), lambda b, l: (0, 0))
    return pl.pallas_call(
        functools.partial(_conv_prompt_kernel, tl=tl),
        grid=(bsz, seq // tl),
        in_specs=[pl.BlockSpec((1, tl, d), lambda b, l: (b, l, 0)),
                  pl.BlockSpec((CONV_WIDTH, d), lambda b, l: (0, 0)), vec, vec, vec],
        out_specs=pl.BlockSpec((1, tl, d), lambda b, l: (b, l, 0)),
        out_shape=jax.ShapeDtypeStruct((bsz, seq, d), _F32),
        scratch_shapes=[pltpu.VMEM((_CONV_HALO + tl, d), _F32)],
        compiler_params=_params("parallel", "arbitrary"),
        name="conv_prompt",
    )(u, w_dw, b_dw.reshape(1, d), ln_g.reshape(1, d), ln_b.reshape(1, d))


def _conv_step_kernel(u_ref, st_ref, wdw_ref, bdw_ref, g_ref, b_ref, o_ref, st_out_ref):
    st = st_ref[0]
    u = u_ref[0]
    hist = CONV_WIDTH - 1
    y = (jnp.sum(st * wdw_ref[:hist, :], axis=0, keepdims=True) + u * wdw_ref[hist:CONV_WIDTH, :]
         + bdw_ref[...])
    y = _layer_norm_rows(y, g_ref[...], b_ref[...])
    o_ref[0] = y * jax.nn.sigmoid(y)
    st_out_ref[0, :hist - 1, :] = st[1:, :]
    st_out_ref[0, hist - 1:, :] = u


def _conv_step(u, state, w_dw, b_dw, ln_g, ln_b):
    bsz, d = u.shape
    hist = CONV_WIDTH - 1
    vec = pl.BlockSpec((1, d), lambda b: (0, 0))
    y, st_new = pl.pallas_call(
        _conv_step_kernel,
        grid=(bsz,),
        in_specs=[pl.BlockSpec((1, 1, d), lambda b: (b, 0, 0)),
                  pl.BlockSpec((1, hist, d), lambda b: (b, 0, 0)),
                  pl.BlockSpec((CONV_WIDTH, d), lambda b: (0, 0)), vec, vec, vec],
        out_specs=[pl.BlockSpec((1, 1, d), lambda b: (b, 0, 0)),
                   pl.BlockSpec((1, hist, d), lambda b: (b, 0, 0))],
        out_shape=[jax.ShapeDtypeStruct((bsz, 1, d), _F32),
                   jax.ShapeDtypeStruct((bsz, hist, d), _F32)],
        compiler_params=_params("parallel"),
        name="conv_step",
    )(u.reshape(bsz, 1, d), state, w_dw, b_dw.reshape(1, d), ln_g.reshape(1, d), ln_b.reshape(1, d))
    return y.reshape(bsz, d), st_new


def _oddeven_merge(lo, hi, r):
    step = r * 2
    if step < hi - lo:
        yield from _oddeven_merge(lo, hi, step)
        yield from _oddeven_merge(lo + r, hi, step)
        yield from [(i, i + r) for i in range(lo + r, hi - r, step)]
    else:
        yield (lo, lo + r)


def _oddeven_sort(lo, hi):
    if hi - lo >= 1:
        mid = lo + (hi - lo) // 2
        yield from _oddeven_sort(lo, mid)
        yield from _oddeven_sort(mid + 1, hi)
        yield from _oddeven_merge(lo, hi, 1)


_SORT_NET = tuple(_oddeven_sort(0, PEER_TOPK - 1))
_MERGE_NET = tuple((i, i + d) for d in (8, 4, 2, 1) for i in range(PEER_TOPK) if i & d == 0)


def _exchange(x, net):
    for i, j in net:
        x[i], x[j] = jnp.maximum(x[i], x[j]), jnp.minimum(x[i], x[j])


def _merge_sublanes(x):
    for shift in (4, 2, 1):
        x = [jnp.maximum(x[i], pltpu.roll(x[PEER_TOPK - 1 - i], shift, 0)) for i in range(PEER_TOPK)]
        _exchange(x, _MERGE_NET)
    return x


def _top16(s):
    x = [s[SUBLANES * v:SUBLANES * (v + 1), :] for v in range(PEER_TOPK)]
    _exchange(x, _SORT_NET)
    return _merge_sublanes(x)


def _peer_route_kernel(x_ref, wq_ref, keys_ref, s1_ref, s2_ref, ea_ref, eb_ref, tau_ref):
    xb = x_ref[...].astype(_BF)
    q_t = lax.dot_general(wq_ref[...], xb, _NT, preferred_element_type=_F32)
    tm = xb.shape[0]
    sub = lax.broadcasted_iota(jnp.int32, (SUBLANES, tm), 0)
    for h in range(PEER_HEADS):
        scores, tops = [], []
        for p in range(2):
            r0 = (h * 2 + p) * PEER_HALF
            q_hp = q_t[r0:r0 + PEER_HALF, :].astype(_BF)
            s = jnp.dot(keys_ref[p, h], q_hp, preferred_element_type=_F32)
            scores.append(s)
            tops.append(_top16(s))
        s1, s2 = scores
        y1, y2 = tops
        v1_lo, v1_hi = y1[0], y1[SUBLANES]
        for a in range(1, SUBLANES):
            v1_lo = jnp.where(sub == a, y1[a], v1_lo)
            v1_hi = jnp.where(sub == a, y1[SUBLANES + a], v1_hi)
        cells = []
        for b in range(PEER_TOPK):
            c = v1_lo + y2[b]
            lim = PEER_TOPK // (b + 1)
            cells.append(c if lim >= SUBLANES else jnp.where(sub < lim, c, _NEG_INF))
        extra = v1_hi + y2[0]
        x = list(cells)
        x[PEER_TOPK - 1] = jnp.maximum(x[PEER_TOPK - 1], extra)
        _exchange(x, _MERGE_NET)
        tau = _merge_sublanes(x)[PEER_TOPK - 1]
        m1, m2 = y1[0], y2[0]
        e1_lo = jnp.exp(v1_lo - m1)
        z = jnp.where(extra >= tau, jnp.exp(v1_hi - m1), 0.0)
        for b in range(PEER_TOPK):
            z = z + jnp.where(cells[b] >= tau, e1_lo * jnp.exp(y2[b] - m2), 0.0)
        z = jnp.sum(z, axis=0, keepdims=True)
        ea = jnp.exp(s1 - m1[0:1, :]) / z
        eb = jnp.exp(s2 - m2[0:1, :])
        for cb in range(tm // LANES):
            cols = slice(cb * LANES, (cb + 1) * LANES)
            s1_ref[h, cb] = s1[:, cols]
            s2_ref[h, cb] = s2[:, cols]
            ea_ref[h, cb] = ea[:, cols]
            eb_ref[h, cb] = eb[:, cols]
        tau_ref[h:h + 1, :] = tau[0:1, :]


def _peer_expert_kernel(xt_ref, u_ref, vt_ref, s1_ref, s2_ref, ea_ref, eb_ref, tau_ref, o_ref,
                        h0_sc, h1_sc, p0_sc, p1_sc, acc_sc, *, te, ne):
    g = pl.program_id(0)
    e_c = jnp.maximum(g - 2, 0) % ne

    @pl.when(g == 0)
    def _():
        for ref in (h0_sc, h1_sc, p0_sc, p1_sc):
            ref[...] = jnp.zeros_like(ref)

    @pl.when((g < 2) | (e_c == 0))
    def _():
        acc_sc[...] = jnp.zeros_like(acc_sc)

    def gate_tile(h_b, p_b, iis, cb):
        cols = slice(cb * LANES, (cb + 1) * LANES)
        groups = N_KEYS // SUBLANES
        w = [[jnp.zeros((SUBLANES, LANES), _F32) for _ in range(groups)] for _ in iis]
        for h in range(PEER_HEADS):
            tau = jnp.broadcast_to(tau_ref[h:h + 1, cols], (SUBLANES, LANES))
            s1 = [jnp.broadcast_to(s1_ref[h, cb, ii:ii + 1, :], (SUBLANES, LANES)) for ii in iis]
            ea = [jnp.broadcast_to(ea_ref[h, cb, ii:ii + 1, :], (SUBLANES, LANES)) for ii in iis]
            for j in range(groups):
                s2 = s2_ref[h, cb, j * SUBLANES:(j + 1) * SUBLANES, :]
                eb = eb_ref[h, cb, j * SUBLANES:(j + 1) * SUBLANES, :]
                for n in range(len(iis)):
                    w[n][j] = w[n][j] + jnp.where((s1[n] + s2) >= tau, ea[n] * eb, 0.0)
            yield
        for n, ii in enumerate(iis):
            rows = slice(ii * N_KEYS, (ii + 1) * N_KEYS)
            hh = h_b[cb, rows, :]
            act = 0.5 * hh * (1.0 + lax.erf(hh * (2.0 ** -0.5)))
            p_b[cb, rows, :] = (jnp.concatenate(w[n], axis=0) * act).astype(_BF)
        yield

    def step(h_a, h_b, p_b, p_c):
        n_cb = h_b.shape[0]
        per = min(MXU_WIDTH // LANES, n_cb)
        n_ii = te // N_KEYS
        d_rows = acc_sc.shape[0] // n_ii

        def stage_c(ii, cbs, cols):
            out_rows = slice(ii * d_rows, (ii + 1) * d_rows)
            p_grp = jnp.concatenate([p_c[cb] for cb in cbs], axis=1)
            acc_sc[out_rows, cols] += jnp.dot(vt_ref[0, out_rows, :], p_grp, preferred_element_type=_F32)

        def stage_a(ii, cbs, cols):
            rows = slice(ii * N_KEYS, (ii + 1) * N_KEYS)
            hv = jnp.dot(u_ref[rows, :], xt_ref[0, :, cols], preferred_element_type=_F32)
            for k, cb in enumerate(cbs):
                h_a[cb, rows, :] = hv[:, k * LANES:(k + 1) * LANES]

        for ii0 in range(0, n_ii, GATE_KEYS):
            iis = range(ii0, ii0 + GATE_KEYS)
            for grp in range(n_cb // per):
                cbs = range(grp * per, (grp + 1) * per)
                cols = slice(grp * per * LANES, (grp + 1) * per * LANES)
                pieces = [functools.partial(f, ii, cbs, cols) for ii in iis for f in (stage_c, stage_a)]
                chunks = itertools.chain.from_iterable(gate_tile(h_b, p_b, iis, cb) for cb in cbs)
                n_chunks = len(cbs) * (PEER_HEADS + 1)
                every = max(n_chunks // len(pieces), 1)
                for idx in range(n_chunks):
                    if idx % every == 0 and pieces:
                        pieces.pop(0)()
                    next(chunks)
                for piece in pieces:
                    piece()

    @pl.when(g % 2 == 0)
    def _():
        step(h0_sc, h1_sc, p1_sc, p0_sc)

    @pl.when(g % 2 == 1)
    def _():
        step(h1_sc, h0_sc, p0_sc, p1_sc)

    @pl.when((g >= 2) & (e_c == ne - 1))
    def _():
        o_ref[...] = acc_sc[...].T


def _peer(x, wq_t, keys, u_bf, v_tiles, tm=512, tr=256):
    n, d = x.shape
    te = v_tiles.shape[2]
    tm = min(tm, n)
    tr = min(tr, n)
    rows = PEER_HEADS * 2 * PEER_HALF
    sc_spec = pl.BlockSpec((PEER_HEADS, tr // LANES, N_KEYS, LANES), lambda t: (0, t, 0, 0))
    sc_shape = jax.ShapeDtypeStruct((PEER_HEADS, n // LANES, N_KEYS, LANES), _F32)
    s1, s2, ea, eb, tau = pl.pallas_call(
        _peer_route_kernel,
        grid=(n // tr,),
        in_specs=[pl.BlockSpec((tr, d), lambda t: (t, 0)),
                  pl.BlockSpec((rows, d), lambda t: (0, 0)),
                  pl.BlockSpec(keys.shape, lambda t: (0, 0, 0, 0))],
        out_specs=[sc_spec, sc_spec, sc_spec, sc_spec, pl.BlockSpec((PEER_HEADS, tr), lambda t: (0, t))],
        out_shape=[sc_shape, sc_shape, sc_shape, sc_shape, jax.ShapeDtypeStruct((PEER_HEADS, n), _F32)],
        compiler_params=_params("parallel"),
        name="peer_route",
    )(x, wq_t, keys)

    nt = n // tm
    ne = N_EXPERTS // te
    last = nt * ne - 1
    step_a = lambda g: jnp.minimum(g, last)
    tile_b = lambda g: jnp.minimum(jnp.maximum(g - 1, 0) // ne, nt - 1)
    step_c = lambda g: jnp.maximum(g - 2, 0)
    assert (te // N_KEYS) % SUBLANES == 0, "first-half key rows of an expert tile must fill whole sublane tiles"
    n_cb = tm // LANES
    xt_tiles = x.astype(_BF).reshape(nt, tm, d).transpose(0, 2, 1)
    sc_all = pl.BlockSpec((PEER_HEADS, n_cb, N_KEYS, LANES), lambda g: (0, tile_b(g), 0, 0))
    sc_rows = pl.BlockSpec((PEER_HEADS, n_cb, te // N_KEYS, LANES),
                           lambda g: (0, tile_b(g), jnp.maximum(g - 1, 0) % ne, 0))
    return pl.pallas_call(
        functools.partial(_peer_expert_kernel, te=te, ne=ne),
        grid=(nt * ne + 2,),
        in_specs=[pl.BlockSpec((1, d, tm), lambda g: (step_a(g) // ne, 0, 0)),
                  pl.BlockSpec((te, d), lambda g: (step_a(g) % ne, 0)),
                  pl.BlockSpec((1, d, te), lambda g: (step_c(g) % ne, 0, 0)),
                  sc_rows, sc_all, sc_rows, sc_all,
                  pl.BlockSpec((PEER_HEADS, tm), lambda g: (0, tile_b(g)))],
        out_specs=pl.BlockSpec((tm, d), lambda g: (step_c(g) // ne, 0)),
        out_shape=jax.ShapeDtypeStruct((n, d), _F32),
        scratch_shapes=[pltpu.VMEM((n_cb, te, LANES), _F32), pltpu.VMEM((n_cb, te, LANES), _F32),
                        pltpu.VMEM((n_cb, te, LANES), _BF), pltpu.VMEM((n_cb, te, LANES), _BF),
                        pltpu.VMEM((d, tm), _F32)],
        compiler_params=_params("arbitrary"),
        name="peer_experts",
    )(xt_tiles, u_bf, v_tiles, s1, s2, ea, eb, tau)


def _rope_tables(pos):
    inv_freq = ROPE_THETA ** (-jnp.arange(ROPE_HALF, dtype=_F32) / ROPE_HALF)
    ang = pos.astype(_F32)[:, None] * inv_freq[None, :]
    return jnp.tile(jnp.cos(ang), (1, MLA_HEADS)), jnp.tile(jnp.sin(ang), (1, MLA_HEADS))


def _mixer_weights(w_in, w_uq, w_ukv, w_out):
    w_hg = w_in[:, :HG_IN].astype(_BF)
    o = HG_IN
    cq_w = w_in[:, o:o + Q_LORA]
    ckv_w = w_in[:, o + Q_LORA:o + Q_LORA + KV_LORA]
    kr_w = w_in[:, o + Q_LORA + KV_LORA:]
    w_mla = jnp.concatenate([cq_w, ckv_w, jnp.tile(kr_w[:, :ROPE_HALF], (1, MLA_HEADS)),
                             jnp.tile(kr_w[:, ROPE_HALF:], (1, MLA_HEADS))], axis=1).astype(_BF)
    wq = w_uq.reshape(Q_LORA, MLA_HEADS, NOPE_DIM + ROPE_DIM)
    w_q = jnp.concatenate([wq[:, :, :NOPE_DIM].reshape(Q_LORA, -1),
                           wq[:, :, NOPE_DIM:NOPE_DIM + ROPE_HALF].reshape(Q_LORA, -1),
                           wq[:, :, NOPE_DIM + ROPE_HALF:].reshape(Q_LORA, -1)], axis=1).astype(_BF)
    w_k = jnp.transpose(w_ukv[:, :, :NOPE_DIM], (1, 2, 0)).astype(_BF)
    w_v = jnp.transpose(w_ukv[:, :, NOPE_DIM:], (1, 0, 2)).astype(_BF)
    return w_hg, w_mla, w_q, w_k, w_v, w_out[:HG_V_W].astype(_BF), w_out[HG_V_W:].astype(_BF)


def kernel(x_prompt, x_sample, cache_mla_latent, cache_mla_krope, state_hgrn, state_conv, page_table, hg_lb_logits, mix_w_in, mla_q_norm, mla_w_uq, mla_kv_norm, mla_w_ukv, hg_out_norm, mix_w_out, conv_w_pw1, conv_b_pw1, conv_w_dw, conv_b_dw, conv_ln_g, conv_ln_b, conv_w_pw2, conv_b_pw2, peer_w_q, peer_sub_keys, peer_u, peer_v, ln_g, ln_b):
    bp, lp, d = x_prompt.shape
    bs, ls, _ = x_sample.shape
    assert ls == 1, "the sample group decodes one token per sequence"
    past_len = page_table.shape[1] * PAGE_SIZE
    lb_all = jnp.cumsum(jax.nn.softmax(hg_lb_logits.astype(_F32), axis=0), axis=0)
    head_mask = (jnp.arange(LANES)[None, :] // ROPE_HALF == jnp.arange(MLA_HEADS)[:, None]).astype(_F32)
    cos_p, sin_p = _rope_tables(jnp.arange(lp))
    cos_s, sin_s = _rope_tables(jnp.full((bs,), past_len))

    hp = x_prompt.reshape(bp * lp, d)
    hs = x_sample.reshape(bs, d)
    outs_p = {"lat": [], "kpe": [], "hg": [], "conv": []}
    outs_s = {"lat": [], "kpe": [], "hg": [], "conv": []}
    for layer in range(DEPTH):
        if layer % 2 == 0:
            e = layer // 2
            lb = lb_all[layer]
            w_hg, w_mla, w_q, w_k, w_v, w_out_hg, w_out_mla = _mixer_weights(
                mix_w_in[e], mla_w_uq[e], mla_w_ukv[e], mix_w_out[e])
            hg = _linear([hp], [w_hg]).reshape(bp, lp, HG_IN)
            o_hg, s_new = _hgrn_prompt(hg, lb, hg_out_norm[e])
            qcat, kcat, c_kv, k_pe = _mla_prep(hp.reshape(bp, lp, d), w_mla, mla_q_norm[e], w_q, mla_kv_norm[e],
                                               w_k, cos_p, sin_p, head_mask)
            o_mla = _mla_prompt_attn(qcat, kcat, w_v)
            mp = _linear([o_hg.reshape(bp * lp, HG_V_W), o_mla.reshape(bp * lp, MLA_HEADS * V_DIM)],
                         [w_out_hg, w_out_mla])
            outs_p["lat"].append(c_kv)
            outs_p["kpe"].append(k_pe)
            outs_p["hg"].append(s_new)
            hg = _linear([hs], [w_hg])
            o_hg, s_new = _hgrn_step(hg, state_hgrn[e], lb, hg_out_norm[e])
            qcat, kcat, c_kv, k_pe = _mla_prep(hs.reshape(1, bs, d), w_mla, mla_q_norm[e], w_q, mla_kv_norm[e],
                                               w_k, cos_s, sin_s, head_mask)
            o_mla = _mla_decode_attn(jnp.transpose(qcat[0], (1, 0, 2)), kcat.reshape(bs, 1, MLA_KW),
                                     cache_mla_latent[e], cache_mla_krope[e], page_table, w_v)
            ms = _linear([o_hg, o_mla], [w_out_hg, w_out_mla])
            outs_s["lat"].append(c_kv.reshape(bs, 1, KV_LORA))
            outs_s["kpe"].append(k_pe.reshape(bs, 1, ROPE_DIM))
            outs_s["hg"].append(s_new)
        else:
            o = layer // 2
            w1 = conv_w_pw1[o].astype(_BF)
            w2 = conv_w_pw2[o].astype(_BF)
            u = _glu_linear(hp, w1, conv_b_pw1[o]).reshape(bp, lp, CONV_DIM)
            y = _conv_prompt(u, conv_w_dw[o], conv_b_dw[o], conv_ln_g[o], conv_ln_b[o])
            mp = _linear([y.reshape(bp * lp, CONV_DIM)], [w2], conv_b_pw2[o])
            outs_p["conv"].append(u[:, lp - (CONV_WIDTH - 1):, :])
            u = _glu_linear(hs, w1, conv_b_pw1[o])
            y, st_new = _conv_step(u, state_conv[o], conv_w_dw[o], conv_b_dw[o], conv_ln_g[o], conv_ln_b[o])
            ms = _linear([y], [w2], conv_b_pw2[o])
            outs_s["conv"].append(st_new)
        hp = _add_ln(hp, mp, ln_g[layer, 0], ln_b[layer, 0])
        hs = _add_ln(hs, ms, ln_g[layer, 0], ln_b[layer, 0])
        wq_t = peer_w_q[layer].T.astype(_BF)
        keys = peer_sub_keys[layer].astype(_BF)
        u_bf = peer_u[layer].astype(_BF)
        v_tiles = peer_v[layer].astype(_BF).reshape(N_EXPERTS // PEER_TE, PEER_TE, d).transpose(0, 2, 1)
        hp = _add_ln(hp, _peer(hp, wq_t, keys, u_bf, v_tiles), ln_g[layer, 1], ln_b[layer, 1])
        hs = _add_ln(hs, _peer(hs, wq_t, keys, u_bf, v_tiles), ln_g[layer, 1], ln_b[layer, 1])
    stack = lambda xs: jnp.stack(xs)
    return (hp.reshape(bp, lp, d), hs.reshape(bs, ls, d),
            stack(outs_p["lat"]), stack(outs_p["kpe"]), stack(outs_p["hg"]), stack(outs_p["conv"]),
            stack(outs_s["lat"]), stack(outs_s["kpe"]), stack(outs_s["hg"]), stack(outs_s["conv"]))
```

```python
import functools
import itertools
import math

import jax
import jax.numpy as jnp
from jax import lax
from jax.experimental import pallas as pl
from jax.experimental.pallas import tpu as pltpu

D_MODEL = 1024
DEPTH = 2
PAGE_SIZE = 128
HG_HEADS = 4
HG_DK = 128
HG_DV = 128
HG_CHUNK = 64
HG_SUB = 8
MLA_HEADS = 4
Q_LORA = 384
KV_LORA = 256
NOPE_DIM = 128
ROPE_DIM = 64
ROPE_HALF = ROPE_DIM // 2
V_DIM = 128
ROPE_THETA = 10000.0
MLA_SCALE = (NOPE_DIM + ROPE_DIM) ** -0.5
CONV_DIM = D_MODEL
CONV_WIDTH = 31
PEER_HEADS = 8
N_KEYS = 128
N_EXPERTS = N_KEYS * N_KEYS
PEER_HALF = 128
PEER_TOPK = 16
PEER_TE = 1024
GATE_KEYS = 1
DN_ALPHA = (2 * DEPTH) ** 0.25
LN_EPS = 1e-5
RMS_EPS = 1e-6
HG_QK_W = HG_HEADS * HG_DK
HG_V_W = HG_HEADS * HG_DV
HG_IN = 2 * HG_QK_W + 2 * HG_V_W
MLA_KW = KV_LORA + 2 * 128

LANES = 128
SUBLANES = 8
MXU_WIDTH = 256
VMEM_LIMIT = 56 * 1024 * 1024

_BF = jnp.bfloat16
_F32 = jnp.float32
_NT = (((1,), (1,)), ((), ()))
_TN = (((0,), (0,)), ((), ()))
_NEG_INF = float("-inf")


def _params(*sem):
    return pltpu.CompilerParams(dimension_semantics=sem, vmem_limit_bytes=VMEM_LIMIT)


def _linear_kernel(*refs, n_in, has_bias):
    o_ref = refs[-1]
    acc = None
    for x_ref, w_ref in zip(refs[:n_in], refs[n_in:2 * n_in]):
        d = jnp.dot(x_ref[...].astype(_BF), w_ref[...], preferred_element_type=_F32)
        acc = d if acc is None else acc + d
    if has_bias:
        acc = acc + refs[2 * n_in][...]
    o_ref[...] = acc


def _linear(xs, ws, bias=None, tm=512):
    m = xs[0].shape[0]
    n = ws[0].shape[1]
    tm = min(tm, m)
    in_specs = [pl.BlockSpec((tm, x.shape[1]), lambda i: (i, 0)) for x in xs]
    in_specs += [pl.BlockSpec(w.shape, lambda i: (0, 0)) for w in ws]
    args = list(xs) + list(ws)
    if bias is not None:
        in_specs.append(pl.BlockSpec((1, n), lambda i: (0, 0)))
        args.append(bias.reshape(1, n))
    return pl.pallas_call(
        functools.partial(_linear_kernel, n_in=len(xs), has_bias=bias is not None),
        grid=(m // tm,),
        in_specs=in_specs,
        out_specs=pl.BlockSpec((tm, n), lambda i: (i, 0)),
        out_shape=jax.ShapeDtypeStruct((m, n), _F32),
        compiler_params=_params("parallel"),
        name="linear",
    )(*args)


def _glu_kernel(x_ref, wa_ref, wb_ref, ba_ref, bb_ref, o_ref):
    xb = x_ref[...].astype(_BF)
    a = jnp.dot(xb, wa_ref[...], preferred_element_type=_F32) + ba_ref[...]
    g = jnp.dot(xb, wb_ref[...], preferred_element_type=_F32) + bb_ref[...]
    o_ref[...] = a * jax.nn.sigmoid(g)


def _glu_linear(x, w, b, tm=512):
    m, k = x.shape
    n = w.shape[1] // 2
    tm = min(tm, m)
    b2 = b.reshape(1, 2 * n)
    return pl.pallas_call(
        _glu_kernel,
        grid=(m // tm,),
        in_specs=[pl.BlockSpec((tm, k), lambda i: (i, 0)),
                  pl.BlockSpec((k, n), lambda i: (0, 0)),
                  pl.BlockSpec((k, n), lambda i: (0, 1)),
                  pl.BlockSpec((1, n), lambda i: (0, 0)),
                  pl.BlockSpec((1, n), lambda i: (0, 1))],
        out_specs=pl.BlockSpec((tm, n), lambda i: (i, 0)),
        out_shape=jax.ShapeDtypeStruct((m, n), _F32),
        compiler_params=_params("parallel"),
        name="glu_linear",
    )(x, w, w, b2, b2)


def _layer_norm_rows(v, g, b):
    mu = jnp.mean(v, axis=-1, keepdims=True)
    c = v - mu
    var = jnp.mean(c * c, axis=-1, keepdims=True)
    return c * lax.rsqrt(var + LN_EPS) * g + b


def _add_ln_kernel(x_ref, f_ref, g_ref, b_ref, o_ref):
    o_ref[...] = _layer_norm_rows(DN_ALPHA * x_ref[...] + f_ref[...], g_ref[...], b_ref[...])


def _add_ln(x, f, g, b, tm=512):
    m, d = x.shape
    tm = min(tm, m)
    row = pl.BlockSpec((tm, d), lambda i: (i, 0))
    vec = pl.BlockSpec((1, d), lambda i: (0, 0))
    return pl.pallas_call(
        _add_ln_kernel,
        grid=(m // tm,),
        in_specs=[row, row, vec, vec],
        out_specs=row,
        out_shape=jax.ShapeDtypeStruct((m, d), _F32),
        compiler_params=_params("parallel"),
        name="add_ln",
    )(x, f, g.reshape(1, d), b.reshape(1, d))


def _hg_out(o, gate, norm_row):
    y = o * lax.rsqrt(jnp.mean(o * o, axis=-1, keepdims=True) + RMS_EPS) * norm_row
    return y * (gate * jax.nn.sigmoid(gate))


def _hgrn_prompt_kernel(q_ref, f_ref, i_ref, g_ref, lb_ref, norm_ref, o_ref, s_ref, st_sc):
    c = HG_CHUNK
    step = pl.program_id(1)

    @pl.when(step == 0)
    def _():
        st_sc[...] = jnp.zeros_like(st_sc)

    row = lax.broadcasted_iota(jnp.int32, (c, c), 0)
    col = lax.broadcasted_iota(jnp.int32, (c, c), 1)
    tril = (col <= row).astype(_F32)
    sub_row = lax.broadcasted_iota(jnp.int32, (HG_SUB, HG_DK), 0)
    norm_row = norm_ref[...]

    heads = range(HG_HEADS)
    sls = [slice(h * HG_DK, (h + 1) * HG_DK) for h in heads]
    lb = [lb_ref[:, sl] for sl in sls]
    q = [q_ref[0, :, sl] for sl in sls]
    z = [f_ref[0, :, sl] for sl in sls]
    v = [i_ref[0, :, sl] for sl in sls]
    log_f = [jnp.log(lb[h] + (1.0 - lb[h]) * jax.nn.sigmoid(z[h])) for h in heads]
    k = [(1.0 - lb[h]) * jax.nn.sigmoid(-z[h]) for h in heads]
    b = [jnp.dot(tril, log_f[h], preferred_element_type=_F32, precision=lax.Precision.HIGHEST) for h in heads]
    b_last = [b[h][c - 1:c, :] for h in heads]
    st = [st_sc[h] for h in heads]
    v_bf = [v[h].astype(_BF) for h in heads]
    o = [lax.dot_general((q[h] * jnp.exp(b[h])).astype(_BF), st[h].astype(_BF), _NT, preferred_element_type=_F32)
         for h in heads]
    blocks = range(c // HG_SUB)
    rows = [slice(blk * HG_SUB, (blk + 1) * HG_SUB) for blk in blocks]
    a_off = {}
    for blk in blocks[1:]:
        r0 = blk * HG_SUB
        for h in heads:
            ref_row = b[h][r0 - 1:r0, :]
            qd = (q[h][rows[blk]] * jnp.exp(b[h][rows[blk]] - ref_row)).astype(_BF)
            kd = (k[h][:r0] * jnp.exp(ref_row - b[h][:r0])).astype(_BF)
            a_off[blk, h] = lax.dot_general(qd, kd, _NT, preferred_element_type=_F32)
    a_col = {}
    for blk in blocks:
        for s in range(HG_SUB):
            for h in heads:
                bb = b[h][rows[blk]]
                w = jnp.where(sub_row >= s, jnp.exp(bb - bb[s:s + 1, :]), 0.0)
                k_row = k[h][blk * HG_SUB + s:blk * HG_SUB + s + 1, :]
                a_col[blk, s, h] = jnp.sum(q[h][rows[blk]] * k_row * w, axis=-1, keepdims=True)
    off = {(blk, h): jnp.dot(a_off[blk, h].astype(_BF), v_bf[h][:blk * HG_SUB], preferred_element_type=_F32)
           for blk in blocks[1:] for h in heads}
    for h in heads:
        pieces = []
        for blk in blocks:
            ob = o[h][rows[blk]]
            if blk > 0:
                ob = ob + off[blk, h]
            for s in range(HG_SUB):
                ob = ob + a_col[blk, s, h] * v[h][blk * HG_SUB + s:blk * HG_SUB + s + 1, :]
            pieces.append(ob)
        o_ref[0, :, sls[h]] = _hg_out(jnp.concatenate(pieces, axis=0), g_ref[0, :, sls[h]], norm_row)
    st_new = []
    for h in heads:
        kd_all = (k[h] * jnp.exp(b_last[h] - b[h])).astype(_BF)
        st_new.append(st[h] * jnp.exp(b_last[h])
                      + jnp.dot(v[h].T.astype(_BF), kd_all, preferred_element_type=_F32))
        st_sc[h] = st_new[h]

    @pl.when(step == pl.num_programs(1) - 1)
    def _():
        for h in heads:
            s_ref[0, h] = st_new[h].T


def _hgrn_prompt(hg, lb, out_norm):
    bsz, seq, _ = hg.shape
    c = HG_CHUNK
    w = HG_QK_W
    blk = lambda j: pl.BlockSpec((1, c, w), lambda b, l, j=j: (b, l, j))
    return pl.pallas_call(
        _hgrn_prompt_kernel,
        grid=(bsz, seq // c),
        in_specs=[blk(0), blk(1), blk(2), blk(3),
                  pl.BlockSpec((1, w), lambda b, l: (0, 0)),
                  pl.BlockSpec((1, HG_DV), lambda b, l: (0, 0))],
        out_specs=[pl.BlockSpec((1, c, HG_V_W), lambda b, l: (b, l, 0)),
                   pl.BlockSpec((1, HG_HEADS, HG_DK, HG_DV), lambda b, l: (b, 0, 0, 0))],
        out_shape=[jax.ShapeDtypeStruct((bsz, seq, HG_V_W), _F32),
                   jax.ShapeDtypeStruct((bsz, HG_HEADS, HG_DK, HG_DV), _F32)],
        scratch_shapes=[pltpu.VMEM((HG_HEADS, HG_DV, HG_DK), _F32)],
        compiler_params=_params("parallel", "arbitrary"),
        name="hgrn_prompt",
    )(hg, hg, hg, hg, lb.reshape(1, w), out_norm.reshape(1, HG_DV))


def _hgrn_step_kernel(cols_ref, row_ref, lbc_ref, norm_ref, s_ref, o_ref, s_out_ref):
    norm_row = norm_ref[...]
    for h in range(HG_HEADS):
        sl_v = slice(2 * HG_QK_W + h * HG_DV, 2 * HG_QK_W + (h + 1) * HG_DV)
        sl_g = slice(2 * HG_QK_W + HG_V_W + h * HG_DV, 2 * HG_QK_W + HG_V_W + (h + 1) * HG_DV)
        qc = cols_ref[0, h, :, 0:1]
        zc = cols_ref[0, h, :, 1:2]
        lbc = lbc_ref[h]
        f = lbc + (1.0 - lbc) * jax.nn.sigmoid(zc)
        kc = (1.0 - lbc) * jax.nn.sigmoid(-zc)
        v_row = row_ref[0, :, sl_v]
        s_new = f * s_ref[0, h] + kc * v_row
        s_out_ref[0, h] = s_new
        o = jnp.sum(qc * s_new, axis=0, keepdims=True)
        o_ref[0, :, h * HG_DV:(h + 1) * HG_DV] = _hg_out(o, row_ref[0, :, sl_g], norm_row)


def _hgrn_step(hg, state, lb, out_norm):
    bsz = hg.shape[0]
    cols = hg[:, :2 * HG_QK_W].reshape(bsz, 2, HG_HEADS, HG_DK).transpose(0, 2, 3, 1)
    rows = hg.reshape(bsz, 1, HG_IN)
    lbc = lb.reshape(HG_HEADS, HG_DK, 1)
    st_spec = pl.BlockSpec((1, HG_HEADS, HG_DK, HG_DV), lambda b: (b, 0, 0, 0))
    o, s_new = pl.pallas_call(
        _hgrn_step_kernel,
        grid=(bsz,),
        in_specs=[pl.BlockSpec((1, HG_HEADS, HG_DK, 2), lambda b: (b, 0, 0, 0)),
                  pl.BlockSpec((1, 1, HG_IN), lambda b: (b, 0, 0)),
                  pl.BlockSpec((HG_HEADS, HG_DK, 1), lambda b: (0, 0, 0)),
                  pl.BlockSpec((1, HG_DV), lambda b: (0, 0)),
                  st_spec],
        out_specs=[pl.BlockSpec((1, 1, HG_V_W), lambda b: (b, 0, 0)), st_spec],
        out_shape=[jax.ShapeDtypeStruct((bsz, 1, HG_V_W), _F32),
                   jax.ShapeDtypeStruct(state.shape, _F32)],
        compiler_params=_params("parallel"),
        name="hgrn_step",
    )(cols, rows, lbc, out_norm.reshape(1, HG_DV), state)
    return o.reshape(bsz, HG_V_W), s_new


def _rms_rows(x, g):
    return x * lax.rsqrt(jnp.mean(x * x, axis=-1, keepdims=True) + RMS_EPS) * g


def _mla_prep_kernel(x_ref, win_ref, qn_ref, wq_ref, kvn_ref, wk_ref, cos_ref, sin_ref, hm_ref,
                     qcat_ref, kcat_ref, ckv_ref, kpe_ref):
    m = jnp.dot(x_ref[0].astype(_BF), win_ref[...], preferred_element_type=_F32)
    cq = _rms_rows(m[:, :Q_LORA], qn_ref[...])
    c_kv = _rms_rows(m[:, Q_LORA:Q_LORA + KV_LORA], kvn_ref[...])
    kr1 = m[:, Q_LORA + KV_LORA:Q_LORA + KV_LORA + LANES]
    kr2 = m[:, Q_LORA + KV_LORA + LANES:]
    cos = cos_ref[...]
    sin = sin_ref[...]
    k1 = kr1 * cos - kr2 * sin
    k2 = kr1 * sin + kr2 * cos
    ckv_ref[0] = c_kv
    lane = lax.broadcasted_iota(jnp.int32, k1.shape, 1)
    kpe_ref[0] = jnp.where(lane < ROPE_HALF, k1, k2)[:, :ROPE_DIM]
    kcat_ref[0, :, :KV_LORA] = c_kv.astype(_BF)
    kcat_ref[0, :, KV_LORA:KV_LORA + LANES] = k1.astype(_BF)
    kcat_ref[0, :, KV_LORA + LANES:] = k2.astype(_BF)

    qh = jnp.dot(cq.astype(_BF), wq_ref[...], preferred_element_type=_F32)
    nope_w = MLA_HEADS * NOPE_DIM
    q1 = qh[:, nope_w:nope_w + LANES]
    q2 = qh[:, nope_w + LANES:]
    r1 = q1 * cos - q2 * sin
    r2 = q1 * sin + q2 * cos
    for h in range(MLA_HEADS):
        q_nope = qh[:, h * NOPE_DIM:(h + 1) * NOPE_DIM].astype(_BF)
        q_abs = jnp.dot(q_nope, wk_ref[h], preferred_element_type=_F32)
        hm = hm_ref[h:h + 1, :]
        qcat_ref[0, h, :, :KV_LORA] = q_abs.astype(_BF)
        qcat_ref[0, h, :, KV_LORA:KV_LORA + LANES] = (r1 * hm).astype(_BF)
        qcat_ref[0, h, :, KV_LORA + LANES:] = (r2 * hm).astype(_BF)


def _mla_prep(x, w_in, q_norm, w_q, kv_norm, w_k, cos, sin, head_mask, tm=256):
    bsz, seq, d = x.shape
    tm = min(tm, seq)
    full = lambda a: pl.BlockSpec(a.shape, lambda b, l: (0,) * a.ndim)
    qn = q_norm.reshape(1, Q_LORA)
    kvn = kv_norm.reshape(1, KV_LORA)
    return pl.pallas_call(
        _mla_prep_kernel,
        grid=(bsz, seq // tm),
        in_specs=[pl.BlockSpec((1, tm, d), lambda b, l: (b, l, 0)),
                  full(w_in), full(qn), full(w_q), full(kvn), full(w_k),
                  pl.BlockSpec((tm, LANES), lambda b, l: (l, 0)),
                  pl.BlockSpec((tm, LANES), lambda b, l: (l, 0)),
                  full(head_mask)],
        out_specs=[pl.BlockSpec((1, MLA_HEADS, tm, MLA_KW), lambda b, l: (b, 0, l, 0)),
                   pl.BlockSpec((1, tm, MLA_KW), lambda b, l: (b, l, 0)),
                   pl.BlockSpec((1, tm, KV_LORA), lambda b, l: (b, l, 0)),
                   pl.BlockSpec((1, tm, ROPE_DIM), lambda b, l: (b, l, 0))],
        out_shape=[jax.ShapeDtypeStruct((bsz, MLA_HEADS, seq, MLA_KW), _BF),
                   jax.ShapeDtypeStruct((bsz, seq, MLA_KW), _BF),
                   jax.ShapeDtypeStruct((bsz, seq, KV_LORA), _F32),
                   jax.ShapeDtypeStruct((bsz, seq, ROPE_DIM), _F32)],
        compiler_params=_params("parallel", "parallel"),
        name="mla_prep",
    )(x, w_in, qn, w_q, kvn, w_k, cos, sin, head_mask)


def _mla_attn_kernel(q_ref, k_ref, wv_ref, o_ref, m_sc, l_sc, acc_sc, *, tq, tk):
    qi = pl.program_id(1)
    kj = pl.program_id(2)
    rows = MLA_HEADS * tq

    @pl.when(kj == 0)
    def _():
        m_sc[...] = jnp.full_like(m_sc, _NEG_INF)
        l_sc[...] = jnp.zeros_like(l_sc)
        acc_sc[...] = jnp.zeros_like(acc_sc)

    def update(masked):
        q = q_ref[0].reshape(rows, MLA_KW)
        k = k_ref[0]
        s = lax.dot_general(q, k, _NT, preferred_element_type=_F32) * MLA_SCALE
        if masked:
            q_pos = qi * tq + (lax.broadcasted_iota(jnp.int32, (rows, tk), 0) & (tq - 1))
            k_pos = kj * tk + lax.broadcasted_iota(jnp.int32, (rows, tk), 1)
            s = jnp.where(k_pos <= q_pos, s, _NEG_INF)
        m_old = m_sc[...]
        m_new = jnp.maximum(m_old, jnp.max(s, axis=-1, keepdims=True))
        alpha = jnp.exp(m_old - m_new)
        p = jnp.exp(s - m_new)
        l_sc[...] = alpha * l_sc[...] + jnp.sum(p, axis=-1, keepdims=True)
        acc_sc[...] = alpha * acc_sc[...] + jnp.dot(p.astype(_BF), k[:, :KV_LORA], preferred_element_type=_F32)
        m_sc[...] = m_new

    last_key = kj * tk + (tk - 1)

    @pl.when(last_key <= qi * tq)
    def _():
        update(masked=False)

    @pl.when((last_key > qi * tq) & (kj * tk <= qi * tq + (tq - 1)))
    def _():
        update(masked=True)

    @pl.when(kj == pl.num_programs(2) - 1)
    def _():
        o_lat = acc_sc[...] / l_sc[...]
        for h in range(MLA_HEADS):
            oh = o_lat[h * tq:(h + 1) * tq].astype(_BF)
            o_ref[0, :, h * V_DIM:(h + 1) * V_DIM] = jnp.dot(oh, wv_ref[h], preferred_element_type=_F32)


def _mla_prompt_attn(qcat, kcat, w_v, tq=256, tk=512):
    bsz, _, seq, _ = qcat.shape
    tq = min(tq, seq)
    tk = min(tk, seq)
    assert tq & (tq - 1) == 0, "query tile must be a power of two (row -> position uses a bit mask)"

    def k_index(b, i, j):
        return (b, jnp.minimum(j, (i * tq + tq - 1) // tk), 0)

    return pl.pallas_call(
        functools.partial(_mla_attn_kernel, tq=tq, tk=tk),
        grid=(bsz, seq // tq, seq // tk),
        in_specs=[pl.BlockSpec((1, MLA_HEADS, tq, MLA_KW), lambda b, i, j: (b, 0, i, 0)),
                  pl.BlockSpec((1, tk, MLA_KW), k_index),
                  pl.BlockSpec(w_v.shape, lambda b, i, j: (0, 0, 0))],
        out_specs=pl.BlockSpec((1, tq, MLA_HEADS * V_DIM), lambda b, i, j: (b, i, 0)),
        out_shape=jax.ShapeDtypeStruct((bsz, seq, MLA_HEADS * V_DIM), _F32),
        scratch_shapes=[pltpu.VMEM((MLA_HEADS * tq, 1), _F32),
                        pltpu.VMEM((MLA_HEADS * tq, 1), _F32),
                        pltpu.VMEM((MLA_HEADS * tq, KV_LORA), _F32)],
        compiler_params=_params("parallel", "parallel", "arbitrary"),
        name="mla_prompt_attn",
    )(qcat, kcat, w_v)


def _mla_decode_kernel(pt_ref, q_ref, knew_ref, wv_ref, fold_ref, *refs, pages):
    lat_refs = refs[:pages]
    kr_refs = refs[pages:2 * pages]
    o_ref = refs[2 * pages]
    m_sc, l_sc, acc_sc = refs[2 * pages + 1:]
    g = pl.program_id(1)

    @pl.when(g == 0)
    def _():
        m_sc[...] = jnp.full_like(m_sc, _NEG_INF)
        l_sc[...] = jnp.zeros_like(l_sc)
        acc_sc[...] = jnp.zeros_like(acc_sc)

    q = q_ref[0]
    q_abs = q[:, :KV_LORA]
    q_pe = jnp.dot(q[:, KV_LORA:], fold_ref[...], preferred_element_type=_F32).astype(_BF)

    lat = [lat_refs[p][0, 0].astype(_BF) for p in range(pages)]
    s = jnp.concatenate(
        [lax.dot_general(q_abs, lat[p], _NT, preferred_element_type=_F32)
         + jnp.dot(q_pe, kr_refs[p][0].astype(_BF), preferred_element_type=_F32)
         for p in range(pages)], axis=1) * MLA_SCALE
    m_old = m_sc[...]
    m_run = jnp.maximum(m_old, jnp.max(s, axis=-1, keepdims=True))
    alpha = jnp.exp(m_old - m_run)
    pr = jnp.exp(s - m_run)
    l_run = alpha * l_sc[...] + jnp.sum(pr, axis=-1, keepdims=True)
    pr = pr.astype(_BF)
    pv = functools.reduce(jnp.add, [jnp.dot(pr[:, p * PAGE_SIZE:(p + 1) * PAGE_SIZE], lat[p],
                                            preferred_element_type=_F32) for p in range(pages)])
    acc = alpha * acc_sc[...] + pv
    m_sc[...], l_sc[...], acc_sc[...] = m_run, l_run, acc

    @pl.when(g == pl.num_programs(1) - 1)
    def _():
        k_new = knew_ref[0].astype(_F32)
        s_self = jnp.sum(q.astype(_F32) * k_new, axis=-1, keepdims=True) * MLA_SCALE
        m_fin = jnp.maximum(m_run, s_self)
        alpha = jnp.exp(m_run - m_fin)
        p_self = jnp.exp(s_self - m_fin)
        l_fin = alpha * l_run + p_self
        c_new = k_new[:, :KV_LORA]
        o_lat = (alpha * acc + p_self.astype(_BF).astype(_F32) * c_new) / l_fin
        for h in range(MLA_HEADS):
            o_ref[0, :, h * V_DIM:(h + 1) * V_DIM] = jnp.dot(
                o_lat[h:h + 1].astype(_BF), wv_ref[h], preferred_element_type=_F32)


def _mla_decode_attn(qcat, kcat, cache_lat, cache_kr, page_table, w_v, pages=16):
    bsz = qcat.shape[0]
    n_pages = page_table.shape[1]
    lat4 = cache_lat.reshape((1,) + cache_lat.shape)
    kr3 = jnp.swapaxes(cache_kr, 1, 2)

    def lat_spec(p):
        return pl.BlockSpec((1, 1, PAGE_SIZE, KV_LORA), lambda b, g, pt: (0, pt[b, g * pages + p], 0, 0))

    def kr_spec(p):
        return pl.BlockSpec((1, ROPE_DIM, PAGE_SIZE), lambda b, g, pt: (pt[b, g * pages + p], 0, 0))

    lane = jnp.arange(2 * LANES)
    dst = (lane // LANES) * ROPE_HALF + lane % ROPE_HALF
    fold = (jnp.arange(ROPE_DIM)[None, :] == dst[:, None]).astype(_BF)
    in_specs = [pl.BlockSpec((1, MLA_HEADS, MLA_KW), lambda b, g, pt: (b, 0, 0)),
                pl.BlockSpec((1, 1, MLA_KW), lambda b, g, pt: (b, 0, 0)),
                pl.BlockSpec(w_v.shape, lambda b, g, pt: (0, 0, 0)),
                pl.BlockSpec(fold.shape, lambda b, g, pt: (0, 0))]
    in_specs += [lat_spec(p) for p in range(pages)]
    in_specs += [kr_spec(p) for p in range(pages)]
    out = pl.pallas_call(
        functools.partial(_mla_decode_kernel, pages=pages),
        grid_spec=pltpu.PrefetchScalarGridSpec(
            num_scalar_prefetch=1,
            grid=(bsz, n_pages // pages),
            in_specs=in_specs,
            out_specs=pl.BlockSpec((1, 1, MLA_HEADS * V_DIM), lambda b, g, pt: (b, 0, 0)),
            scratch_shapes=[pltpu.VMEM((MLA_HEADS, 1), _F32),
                            pltpu.VMEM((MLA_HEADS, 1), _F32),
                            pltpu.VMEM((MLA_HEADS, KV_LORA), _F32)]),
        out_shape=jax.ShapeDtypeStruct((bsz, 1, MLA_HEADS * V_DIM), _F32),
        compiler_params=_params("parallel", "arbitrary"),
        name="mla_decode_attn",
    )(page_table, qcat, kcat, w_v, fold, *([lat4] * pages), *([kr3] * pages))
    return out.reshape(bsz, MLA_HEADS * V_DIM)


_CONV_HALO = 32
_CONV_ROWS = 32


def _conv_prompt_kernel(u_ref, wdw_ref, bdw_ref, g_ref, b_ref, o_ref, ext_sc, sh_sc, *, tl):
    @pl.when(pl.program_id(1) == 0)
    def _():
        ext_sc[:_CONV_HALO, :] = jnp.zeros((_CONV_HALO, CONV_DIM), _F32)

    ext_sc[_CONV_HALO:, :] = u_ref[0]
    first = _CONV_HALO - (CONV_WIDTH - 1)
    span = sh_sc.shape[1]
    for r in range(1, SUBLANES):
        sh_sc[r - 1] = ext_sc[r:r + span, :]
    bias, gain, shift = bdw_ref[...], g_ref[...], b_ref[...]
    for c0 in range(0, tl, _CONV_ROWS):
        y = jnp.zeros((_CONV_ROWS, CONV_DIM), _F32) + bias
        for w in range(CONV_WIDTH):
            r = (first + w) % SUBLANES
            a = first + w - r + c0
            src = ext_sc[a:a + _CONV_ROWS, :] if r == 0 else sh_sc[r - 1, a:a + _CONV_ROWS, :]
            y = y + src * wdw_ref[w:w + 1, :]
        y = _layer_norm_rows(y, gain, shift)
        o_ref[0, c0:c0 + _CONV_ROWS, :] = y * jax.nn.sigmoid(y)
    ext_sc[:_CONV_HALO, :] = ext_sc[tl:tl + _CONV_HALO, :]


def _conv_prompt(u, w_dw, b_dw, ln_g, ln_b, tl=256):
    bsz, seq, d = u.shape
    tl = min(tl, seq)
    vec = pl.BlockSpec((1, d), lambda b, l: (0, 0))
    return pl.pallas_call(
        functools.partial(_conv_prompt_kernel, tl=tl),
        grid=(bsz, seq // tl),
        in_specs=[pl.BlockSpec((1, tl, d), lambda b, l: (b, l, 0)),
                  pl.BlockSpec((CONV_WIDTH, d), lambda b, l: (0, 0)), vec, vec, vec],
        out_specs=pl.BlockSpec((1, tl, d), lambda b, l: (b, l, 0)),
        out_shape=jax.ShapeDtypeStruct((bsz, seq, d), _F32),
        scratch_shapes=[pltpu.VMEM((_CONV_HALO + tl, d), _F32),
                        pltpu.VMEM((SUBLANES - 1, tl + _CONV_HALO - SUBLANES, d), _F32)],
        compiler_params=_params("parallel", "arbitrary"),
        name="conv_prompt",
    )(u, w_dw, b_dw.reshape(1, d), ln_g.reshape(1, d), ln_b.reshape(1, d))


def _conv_step_kernel(u_ref, st_ref, wdw_ref, bdw_ref, g_ref, b_ref, o_ref, st_out_ref):
    st = st_ref[0]
    u = u_ref[0]
    hist = CONV_WIDTH - 1
    y = (jnp.sum(st * wdw_ref[:hist, :], axis=0, keepdims=True) + u * wdw_ref[hist:CONV_WIDTH, :]
         + bdw_ref[...])
    y = _layer_norm_rows(y, g_ref[...], b_ref[...])
    o_ref[0] = y * jax.nn.sigmoid(y)
    st_out_ref[0, :hist - 1, :] = st[1:, :]
    st_out_ref[0, hist - 1:, :] = u


def _conv_step(u, state, w_dw, b_dw, ln_g, ln_b):
    bsz, d = u.shape
    hist = CONV_WIDTH - 1
    vec = pl.BlockSpec((1, d), lambda b: (0, 0))
    y, st_new = pl.pallas_call(
        _conv_step_kernel,
        grid=(bsz,),
        in_specs=[pl.BlockSpec((1, 1, d), lambda b: (b, 0, 0)),
                  pl.BlockSpec((1, hist, d), lambda b: (b, 0, 0)),
                  pl.BlockSpec((CONV_WIDTH, d), lambda b: (0, 0)), vec, vec, vec],
        out_specs=[pl.BlockSpec((1, 1, d), lambda b: (b, 0, 0)),
                   pl.BlockSpec((1, hist, d), lambda b: (b, 0, 0))],
        out_shape=[jax.ShapeDtypeStruct((bsz, 1, d), _F32),
                   jax.ShapeDtypeStruct((bsz, hist, d), _F32)],
        compiler_params=_params("parallel"),
        name="conv_step",
    )(u.reshape(bsz, 1, d), state, w_dw, b_dw.reshape(1, d), ln_g.reshape(1, d), ln_b.reshape(1, d))
    return y.reshape(bsz, d), st_new


def _oddeven_merge(lo, hi, r):
    step = r * 2
    if step < hi - lo:
        yield from _oddeven_merge(lo, hi, step)
        yield from _oddeven_merge(lo + r, hi, step)
        yield from [(i, i + r) for i in range(lo + r, hi - r, step)]
    else:
        yield (lo, lo + r)


def _oddeven_sort(lo, hi):
    if hi - lo >= 1:
        mid = lo + (hi - lo) // 2
        yield from _oddeven_sort(lo, mid)
        yield from _oddeven_sort(mid + 1, hi)
        yield from _oddeven_merge(lo, hi, 1)


_SORT_NET = tuple(_oddeven_sort(0, PEER_TOPK - 1))
_MERGE_NET = tuple((i, i + d) for d in (8, 4, 2, 1) for i in range(PEER_TOPK) if i & d == 0)


def _exchange(x, net):
    for i, j in net:
        x[i], x[j] = jnp.maximum(x[i], x[j]), jnp.minimum(x[i], x[j])


def _merge_sublanes(x):
    for shift in (4, 2, 1):
        x = [jnp.maximum(x[i], pltpu.roll(x[PEER_TOPK - 1 - i], shift, 0)) for i in range(PEER_TOPK)]
        _exchange(x, _MERGE_NET)
    return x


def _top16(s):
    x = [s[SUBLANES * v:SUBLANES * (v + 1), :] for v in range(PEER_TOPK)]
    _exchange(x, _SORT_NET)
    return _merge_sublanes(x)


def _next_value(s, last):
    n_ge = jnp.sum((s >= last).astype(_F32), axis=0, keepdims=True)
    below = jnp.max(jnp.where(s < last, s, _NEG_INF), axis=0, keepdims=True)
    return jnp.where(n_ge > PEER_TOPK, last, below)


def _peer_route_kernel(x_ref, wq_ref, keys_ref, th_ref, s2_ref, ea_ref, eb_ref):
    xb = x_ref[...].astype(_BF)
    q_t = lax.dot_general(wq_ref[...], xb, _NT, preferred_element_type=_F32)
    tm = xb.shape[0]
    sub = lax.broadcasted_iota(jnp.int32, (SUBLANES, tm), 0)
    for h in range(PEER_HEADS):
        scores, tops = [], []
        for p in range(2):
            r0 = (h * 2 + p) * PEER_HALF
            q_hp = q_t[r0:r0 + PEER_HALF, :].astype(_BF)
            s = jnp.dot(keys_ref[p, h], q_hp, preferred_element_type=_F32)
            scores.append(s)
            tops.append(_top16(s))
        s1, s2 = scores
        y1, y2 = tops
        v1_lo, v1_hi = y1[0], y1[SUBLANES]
        for a in range(1, SUBLANES):
            v1_lo = jnp.where(sub == a, y1[a], v1_lo)
            v1_hi = jnp.where(sub == a, y1[SUBLANES + a], v1_hi)
        cells = []
        for b in range(PEER_TOPK):
            c = v1_lo + y2[b]
            lim = PEER_TOPK // (b + 1)
            cells.append(c if lim >= SUBLANES else jnp.where(sub < lim, c, _NEG_INF))
        extra = v1_hi + y2[0]
        x = list(cells)
        x[PEER_TOPK - 1] = jnp.maximum(x[PEER_TOPK - 1], extra)
        _exchange(x, _MERGE_NET)
        tau = _merge_sublanes(x)[PEER_TOPK - 1]
        m1, m2 = y1[0], y2[0]
        e1_lo = jnp.exp(v1_lo - m1)
        z = jnp.where(extra >= tau, jnp.exp(v1_hi - m1), 0.0)
        for b in range(PEER_TOPK):
            z = z + jnp.where(cells[b] >= tau, e1_lo * jnp.exp(y2[b] - m2), 0.0)
        z = jnp.sum(z, axis=0, keepdims=True)
        ea = jnp.exp(s1 - m1[0:1, :]) / z
        eb = jnp.exp(s2 - m2[0:1, :])
        tau_row = tau[0:1, :]
        n_ge = functools.reduce(jnp.add, [(c >= tau).astype(_F32) for c in cells + [extra]])
        below = functools.reduce(jnp.maximum, [jnp.where(c < tau, c, _NEG_INF) for c in cells + [extra]])
        c17 = jnp.where(jnp.sum(n_ge, axis=0, keepdims=True) > PEER_TOPK, tau_row,
                        jnp.max(below, axis=0, keepdims=True))
        c17 = jnp.maximum(c17, jnp.maximum(m1[0:1, :] + _next_value(s2, y2[PEER_TOPK - 1][0:1, :]),
                                           m2[0:1, :] + _next_value(s1, y1[PEER_TOPK - 1][0:1, :])))
        theta = 0.5 * (tau_row + c17) - s1
        for cb in range(tm // LANES):
            cols = slice(cb * LANES, (cb + 1) * LANES)
            th_ref[h, cb] = theta[:, cols]
            s2_ref[h, cb] = s2[:, cols]
            ea_ref[h, cb] = ea[:, cols]
            eb_ref[h, cb] = eb[:, cols]


def _peer_expert_kernel(xt_ref, u_ref, vt_ref, th_ref, s2_ref, ea_ref, eb_ref, o_ref,
                        h0_sc, h1_sc, p0_sc, p1_sc, acc_sc, *, te, ne):
    g = pl.program_id(0)
    e_c = jnp.maximum(g - 2, 0) % ne

    @pl.when(g == 0)
    def _():
        for ref in (h0_sc, h1_sc, p0_sc, p1_sc):
            ref[...] = jnp.zeros_like(ref)

    @pl.when((g < 2) | (e_c == 0))
    def _():
        acc_sc[...] = jnp.zeros_like(acc_sc)

    def gate_tile(h_b, p_b, iis, cb):
        cols = slice(cb * LANES, (cb + 1) * LANES)
        groups = N_KEYS // SUBLANES
        w = [[jnp.zeros((SUBLANES, LANES), _F32) for _ in range(groups)] for _ in iis]
        for h in range(PEER_HEADS):
            th = [jnp.broadcast_to(th_ref[h, cb, ii:ii + 1, :], (SUBLANES, LANES)) for ii in iis]
            ea = [jnp.broadcast_to(ea_ref[h, cb, ii:ii + 1, :], (SUBLANES, LANES)) for ii in iis]
            for j in range(groups):
                s2 = s2_ref[h, cb, j * SUBLANES:(j + 1) * SUBLANES, :]
                eb = eb_ref[h, cb, j * SUBLANES:(j + 1) * SUBLANES, :]
                for n in range(len(iis)):
                    w[n][j] = w[n][j] + jnp.where(s2 >= th[n], ea[n] * eb, 0.0)
            yield
        for n, ii in enumerate(iis):
            rows = slice(ii * N_KEYS, (ii + 1) * N_KEYS)
            hh = h_b[cb, rows, :]
            act = 0.5 * hh * (1.0 + lax.erf(hh * (2.0 ** -0.5)))
            p_b[cb, rows, :] = (jnp.concatenate(w[n], axis=0) * act).astype(_BF)
        yield

    def step(h_a, h_b, p_b, p_c):
        n_cb = h_b.shape[0]
        per = min(MXU_WIDTH // LANES, n_cb)
        n_ii = te // N_KEYS
        d_rows = acc_sc.shape[0] // n_ii

        def stage_c(ii, cbs, cols):
            out_rows = slice(ii * d_rows, (ii + 1) * d_rows)
            p_grp = jnp.concatenate([p_c[cb] for cb in cbs], axis=1)
            acc_sc[out_rows, cols] += jnp.dot(vt_ref[0, out_rows, :], p_grp, preferred_element_type=_F32)

        def stage_a(ii, cbs, cols):
            rows = slice(ii * N_KEYS, (ii + 1) * N_KEYS)
            hv = jnp.dot(u_ref[rows, :], xt_ref[0, :, cols], preferred_element_type=_F32)
            for k, cb in enumerate(cbs):
                h_a[cb, rows, :] = hv[:, k * LANES:(k + 1) * LANES]

        for ii0 in range(0, n_ii, GATE_KEYS):
            iis = range(ii0, ii0 + GATE_KEYS)
            for grp in range(n_cb // per):
                cbs = range(grp * per, (grp + 1) * per)
                cols = slice(grp * per * LANES, (grp + 1) * per * LANES)
                pieces = [functools.partial(f, ii, cbs, cols) for ii in iis for f in (stage_c, stage_a)]
                chunks = itertools.chain.from_iterable(gate_tile(h_b, p_b, iis, cb) for cb in cbs)
                n_chunks = len(cbs) * (PEER_HEADS + 1)
                every = max(n_chunks // len(pieces), 1)
                for idx in range(n_chunks):
                    if idx % every == 0 and pieces:
                        pieces.pop(0)()
                    next(chunks)
                for piece in pieces:
                    piece()

    @pl.when(g % 2 == 0)
    def _():
        step(h0_sc, h1_sc, p1_sc, p0_sc)

    @pl.when(g % 2 == 1)
    def _():
        step(h1_sc, h0_sc, p0_sc, p1_sc)

    @pl.when((g >= 2) & (e_c == ne - 1))
    def _():
        o_ref[...] = acc_sc[...].T


def _peer(x, wq_t, keys, u_bf, v_tiles, tm=512, tr=256):
    n, d = x.shape
    te = v_tiles.shape[2]
    tm = min(tm, n)
    tr = min(tr, n)
    rows = PEER_HEADS * 2 * PEER_HALF
    sc_spec = pl.BlockSpec((PEER_HEADS, tr // LANES, N_KEYS, LANES), lambda t: (0, t, 0, 0))
    sc_shape = jax.ShapeDtypeStruct((PEER_HEADS, n // LANES, N_KEYS, LANES), _F32)
    th, s2, ea, eb = pl.pallas_call(
        _peer_route_kernel,
        grid=(n // tr,),
        in_specs=[pl.BlockSpec((tr, d), lambda t: (t, 0)),
                  pl.BlockSpec((rows, d), lambda t: (0, 0)),
                  pl.BlockSpec(keys.shape, lambda t: (0, 0, 0, 0))],
        out_specs=[sc_spec, sc_spec, sc_spec, sc_spec],
        out_shape=[sc_shape, sc_shape, sc_shape, sc_shape],
        compiler_params=_params("parallel"),
        name="peer_route",
    )(x, wq_t, keys)

    nt = n // tm
    ne = N_EXPERTS // te
    last = nt * ne - 1
    step_a = lambda g: jnp.minimum(g, last)
    tile_b = lambda g: jnp.minimum(jnp.maximum(g - 1, 0) // ne, nt - 1)
    step_c = lambda g: jnp.maximum(g - 2, 0)
    assert (te // N_KEYS) % SUBLANES == 0, "first-half key rows of an expert tile must fill whole sublane tiles"
    n_cb = tm // LANES
    xt_tiles = x.astype(_BF).reshape(nt, tm, d).transpose(0, 2, 1)
    sc_all = pl.BlockSpec((PEER_HEADS, n_cb, N_KEYS, LANES), lambda g: (0, tile_b(g), 0, 0))
    sc_rows = pl.BlockSpec((PEER_HEADS, n_cb, te // N_KEYS, LANES),
                           lambda g: (0, tile_b(g), jnp.maximum(g - 1, 0) % ne, 0))
    return pl.pallas_call(
        functools.partial(_peer_expert_kernel, te=te, ne=ne),
        grid=(nt * ne + 2,),
        in_specs=[pl.BlockSpec((1, d, tm), lambda g: (step_a(g) // ne, 0, 0)),
                  pl.BlockSpec((te, d), lambda g: (step_a(g) % ne, 0)),
                  pl.BlockSpec((1, d, te), lambda g: (step_c(g) % ne, 0, 0)),
                  sc_rows, sc_all, sc_rows, sc_all],
        out_specs=pl.BlockSpec((tm, d), lambda g: (step_c(g) // ne, 0)),
        out_shape=jax.ShapeDtypeStruct((n, d), _F32),
        scratch_shapes=[pltpu.VMEM((n_cb, te, LANES), _F32), pltpu.VMEM((n_cb, te, LANES), _F32),
                        pltpu.VMEM((n_cb, te, LANES), _BF), pltpu.VMEM((n_cb, te, LANES), _BF),
                        pltpu.VMEM((d, tm), _F32)],
        compiler_params=_params("arbitrary"),
        name="peer_experts",
    )(xt_tiles, u_bf, v_tiles, th, s2, ea, eb)


def _rope_tables(pos):
    inv_freq = ROPE_THETA ** (-jnp.arange(ROPE_HALF, dtype=_F32) / ROPE_HALF)
    ang = pos.astype(_F32)[:, None] * inv_freq[None, :]
    return jnp.tile(jnp.cos(ang), (1, MLA_HEADS)), jnp.tile(jnp.sin(ang), (1, MLA_HEADS))


def _mixer_weights(w_in, w_uq, w_ukv, w_out):
    w_hg = w_in[:, :HG_IN].astype(_BF)
    o = HG_IN
    cq_w = w_in[:, o:o + Q_LORA]
    ckv_w = w_in[:, o + Q_LORA:o + Q_LORA + KV_LORA]
    kr_w = w_in[:, o + Q_LORA + KV_LORA:]
    w_mla = jnp.concatenate([cq_w, ckv_w, jnp.tile(kr_w[:, :ROPE_HALF], (1, MLA_HEADS)),
                             jnp.tile(kr_w[:, ROPE_HALF:], (1, MLA_HEADS))], axis=1).astype(_BF)
    wq = w_uq.reshape(Q_LORA, MLA_HEADS, NOPE_DIM + ROPE_DIM)
    w_q = jnp.concatenate([wq[:, :, :NOPE_DIM].reshape(Q_LORA, -1),
                           wq[:, :, NOPE_DIM:NOPE_DIM + ROPE_HALF].reshape(Q_LORA, -1),
                           wq[:, :, NOPE_DIM + ROPE_HALF:].reshape(Q_LORA, -1)], axis=1).astype(_BF)
    w_k = jnp.transpose(w_ukv[:, :, :NOPE_DIM], (1, 2, 0)).astype(_BF)
    w_v = jnp.transpose(w_ukv[:, :, NOPE_DIM:], (1, 0, 2)).astype(_BF)
    return w_hg, w_mla, w_q, w_k, w_v, w_out[:HG_V_W].astype(_BF), w_out[HG_V_W:].astype(_BF)


def kernel(x_prompt, x_sample, cache_mla_latent, cache_mla_krope, state_hgrn, state_conv, page_table, hg_lb_logits, mix_w_in, mla_q_norm, mla_w_uq, mla_kv_norm, mla_w_ukv, hg_out_norm, mix_w_out, conv_w_pw1, conv_b_pw1, conv_w_dw, conv_b_dw, conv_ln_g, conv_ln_b, conv_w_pw2, conv_b_pw2, peer_w_q, peer_sub_keys, peer_u, peer_v, ln_g, ln_b):
    bp, lp, d = x_prompt.shape
    bs, ls, _ = x_sample.shape
    assert ls == 1, "the sample group decodes one token per sequence"
    past_len = page_table.shape[1] * PAGE_SIZE
    lb_all = jnp.cumsum(jax.nn.softmax(hg_lb_logits.astype(_F32), axis=0), axis=0)
    head_mask = (jnp.arange(LANES)[None, :] // ROPE_HALF == jnp.arange(MLA_HEADS)[:, None]).astype(_F32)
    cos_p, sin_p = _rope_tables(jnp.arange(lp))
    cos_s, sin_s = _rope_tables(jnp.full((bs,), past_len))

    hp = x_prompt.reshape(bp * lp, d)
    hs = x_sample.reshape(bs, d)
    outs_p = {"lat": [], "kpe": [], "hg": [], "conv": []}
    outs_s = {"lat": [], "kpe": [], "hg": [], "conv": []}
    for layer in range(DEPTH):
        if layer % 2 == 0:
            e = layer // 2
            lb = lb_all[layer]
            w_hg, w_mla, w_q, w_k, w_v, w_out_hg, w_out_mla = _mixer_weights(
                mix_w_in[e], mla_w_uq[e], mla_w_ukv[e], mix_w_out[e])
            hg = _linear([hp], [w_hg]).reshape(bp, lp, HG_IN)
            o_hg, s_new = _hgrn_prompt(hg, lb, hg_out_norm[e])
            qcat, kcat, c_kv, k_pe = _mla_prep(hp.reshape(bp, lp, d), w_mla, mla_q_norm[e], w_q, mla_kv_norm[e],
                                               w_k, cos_p, sin_p, head_mask)
            o_mla = _mla_prompt_attn(qcat, kcat, w_v)
            mp = _linear([o_hg.reshape(bp * lp, HG_V_W), o_mla.reshape(bp * lp, MLA_HEADS * V_DIM)],
                         [w_out_hg, w_out_mla])
            outs_p["lat"].append(c_kv)
            outs_p["kpe"].append(k_pe)
            outs_p["hg"].append(s_new)
            hg = _linear([hs], [w_hg])
            o_hg, s_new = _hgrn_step(hg, state_hgrn[e], lb, hg_out_norm[e])
            qcat, kcat, c_kv, k_pe = _mla_prep(hs.reshape(1, bs, d), w_mla, mla_q_norm[e], w_q, mla_kv_norm[e],
                                               w_k, cos_s, sin_s, head_mask)
            o_mla = _mla_decode_attn(jnp.transpose(qcat[0], (1, 0, 2)), kcat.reshape(bs, 1, MLA_KW),
                                     cache_mla_latent[e], cache_mla_krope[e], page_table, w_v)
            ms = _linear([o_hg, o_mla], [w_out_hg, w_out_mla])
            outs_s["lat"].append(c_kv.reshape(bs, 1, KV_LORA))
            outs_s["kpe"].append(k_pe.reshape(bs, 1, ROPE_DIM))
            outs_s["hg"].append(s_new)
        else:
            o = layer // 2
            w1 = conv_w_pw1[o].astype(_BF)
            w2 = conv_w_pw2[o].astype(_BF)
            u = _glu_linear(hp, w1, conv_b_pw1[o]).reshape(bp, lp, CONV_DIM)
            y = _conv_prompt(u, conv_w_dw[o], conv_b_dw[o], conv_ln_g[o], conv_ln_b[o])
            mp = _linear([y.reshape(bp * lp, CONV_DIM)], [w2], conv_b_pw2[o])
            outs_p["conv"].append(u[:, lp - (CONV_WIDTH - 1):, :])
            u = _glu_linear(hs, w1, conv_b_pw1[o])
            y, st_new = _conv_step(u, state_conv[o], conv_w_dw[o], conv_b_dw[o], conv_ln_g[o], conv_ln_b[o])
            ms = _linear([y], [w2], conv_b_pw2[o])
            outs_s["conv"].append(st_new)
        hp = _add_ln(hp, mp, ln_g[layer, 0], ln_b[layer, 0])
        hs = _add_ln(hs, ms, ln_g[layer, 0], ln_b[layer, 0])
        wq_t = peer_w_q[layer].T.astype(_BF)
        keys = peer_sub_keys[layer].astype(_BF)
        u_bf = peer_u[layer].astype(_BF)
        v_tiles = peer_v[layer].astype(_BF).reshape(N_EXPERTS // PEER_TE, PEER_TE, d).transpose(0, 2, 1)
        hp = _add_ln(hp, _peer(hp, wq_t, keys, u_bf, v_tiles), ln_g[layer, 1], ln_b[layer, 1])
        hs = _add_ln(hs, _peer(hs, wq_t, keys, u_bf, v_tiles), ln_g[layer, 1], ln_b[layer, 1])
    stack = lambda xs: jnp.stack(xs)
    return (hp.reshape(bp, lp, d), hs.reshape(bs, ls, d),
            stack(outs_p["lat"]), stack(outs_p["kpe"]), stack(outs_p["hg"]), stack(outs_p["conv"]),
            stack(outs_s["lat"]), stack(outs_s["kpe"]), stack(outs_s["hg"]), stack(outs_s["conv"]))
```

```python
import functools
import itertools
import math

import jax
import jax.numpy as jnp
from jax import lax
from jax.experimental import pallas as pl
from jax.experimental.pallas import tpu as pltpu

D_MODEL = 1024
DEPTH = 2
PAGE_SIZE = 128
HG_HEADS = 4
HG_DK = 128
HG_DV = 128
HG_CHUNK = 64
HG_SUB = 8
MLA_HEADS = 4
Q_LORA = 384
KV_LORA = 256
NOPE_DIM = 128
ROPE_DIM = 64
ROPE_HALF = ROPE_DIM // 2
V_DIM = 128
ROPE_THETA = 10000.0
MLA_SCALE = (NOPE_DIM + ROPE_DIM) ** -0.5
CONV_DIM = D_MODEL
CONV_WIDTH = 31
PEER_HEADS = 8
N_KEYS = 128
N_EXPERTS = N_KEYS * N_KEYS
PEER_HALF = 128
PEER_TOPK = 16
PEER_TE = 1024
GATE_KEYS = 1
DN_ALPHA = (2 * DEPTH) ** 0.25
LN_EPS = 1e-5
RMS_EPS = 1e-6
HG_QK_W = HG_HEADS * HG_DK
HG_V_W = HG_HEADS * HG_DV
HG_IN = 2 * HG_QK_W + 2 * HG_V_W
MLA_KW = KV_LORA + 2 * 128

LANES = 128
SUBLANES = 8
MXU_WIDTH = 256
VMEM_LIMIT = 56 * 1024 * 1024

_BF = jnp.bfloat16
_F32 = jnp.float32
_NT = (((1,), (1,)), ((), ()))
_TN = (((0,), (0,)), ((), ()))
_NEG_INF = float("-inf")


def _params(*sem):
    return pltpu.CompilerParams(dimension_semantics=sem, vmem_limit_bytes=VMEM_LIMIT)


def _linear_kernel(*refs, n_in, has_bias):
    o_ref = refs[-1]
    acc = None
    for x_ref, w_ref in zip(refs[:n_in], refs[n_in:2 * n_in]):
        d = jnp.dot(x_ref[...].astype(_BF), w_ref[...], preferred_element_type=_F32)
        acc = d if acc is None else acc + d
    if has_bias:
        acc = acc + refs[2 * n_in][...]
    o_ref[...] = acc


def _linear(xs, ws, bias=None, tm=512):
    m = xs[0].shape[0]
    n = ws[0].shape[1]
    tm = min(tm, m)
    in_specs = [pl.BlockSpec((tm, x.shape[1]), lambda i: (i, 0)) for x in xs]
    in_specs += [pl.BlockSpec(w.shape, lambda i: (0, 0)) for w in ws]
    args = list(xs) + list(ws)
    if bias is not None:
        in_specs.append(pl.BlockSpec((1, n), lambda i: (0, 0)))
        args.append(bias.reshape(1, n))
    return pl.pallas_call(
        functools.partial(_linear_kernel, n_in=len(xs), has_bias=bias is not None),
        grid=(m // tm,),
        in_specs=in_specs,
        out_specs=pl.BlockSpec((tm, n), lambda i: (i, 0)),
        out_shape=jax.ShapeDtypeStruct((m, n), _F32),
        compiler_params=_params("parallel"),
        name="linear",
    )(*args)


def _glu_kernel(x_ref, wa_ref, wb_ref, ba_ref, bb_ref, o_ref):
    xb = x_ref[...].astype(_BF)
    a = jnp.dot(xb, wa_ref[...], preferred_element_type=_F32) + ba_ref[...]
    g = jnp.dot(xb, wb_ref[...], preferred_element_type=_F32) + bb_ref[...]
    o_ref[...] = a * jax.nn.sigmoid(g)


def _glu_linear(x, w, b, tm=512):
    m, k = x.shape
    n = w.shape[1] // 2
    tm = min(tm, m)
    b2 = b.reshape(1, 2 * n)
    return pl.pallas_call(
        _glu_kernel,
        grid=(m // tm,),
        in_specs=[pl.BlockSpec((tm, k), lambda i: (i, 0)),
                  pl.BlockSpec((k, n), lambda i: (0, 0)),
                  pl.BlockSpec((k, n), lambda i: (0, 1)),
                  pl.BlockSpec((1, n), lambda i: (0, 0)),
                  pl.BlockSpec((1, n), lambda i: (0, 1))],
        out_specs=pl.BlockSpec((tm, n), lambda i: (i, 0)),
        out_shape=jax.ShapeDtypeStruct((m, n), _F32),
        compiler_params=_params("parallel"),
        name="glu_linear",
    )(x, w, w, b2, b2)


def _layer_norm_rows(v, g, b):
    mu = jnp.mean(v, axis=-1, keepdims=True)
    c = v - mu
    var = jnp.mean(c * c, axis=-1, keepdims=True)
    return c * lax.rsqrt(var + LN_EPS) * g + b


def _add_ln_kernel(x_ref, f_ref, g_ref, b_ref, o_ref):
    o_ref[...] = _layer_norm_rows(DN_ALPHA * x_ref[...] + f_ref[...], g_ref[...], b_ref[...])


def _add_ln(x, f, g, b, tm=512):
    m, d = x.shape
    tm = min(tm, m)
    row = pl.BlockSpec((tm, d), lambda i: (i, 0))
    vec = pl.BlockSpec((1, d), lambda i: (0, 0))
    return pl.pallas_call(
        _add_ln_kernel,
        grid=(m // tm,),
        in_specs=[row, row, vec, vec],
        out_specs=row,
        out_shape=jax.ShapeDtypeStruct((m, d), _F32),
        compiler_params=_params("parallel"),
        name="add_ln",
    )(x, f, g.reshape(1, d), b.reshape(1, d))


def _hg_out(o, gate, norm_row):
    y = o * lax.rsqrt(jnp.mean(o * o, axis=-1, keepdims=True) + RMS_EPS) * norm_row
    return y * (gate * jax.nn.sigmoid(gate))


def _hgrn_prompt_kernel(q_ref, f_ref, i_ref, g_ref, lb_ref, norm_ref, o_ref, s_ref, st_sc):
    c = HG_CHUNK
    step = pl.program_id(1)

    @pl.when(step == 0)
    def _():
        st_sc[...] = jnp.zeros_like(st_sc)

    row = lax.broadcasted_iota(jnp.int32, (c, c), 0)
    col = lax.broadcasted_iota(jnp.int32, (c, c), 1)
    tril = (col <= row).astype(_F32)
    sub_row = lax.broadcasted_iota(jnp.int32, (HG_SUB, HG_DK), 0)
    norm_row = norm_ref[...]

    heads = range(HG_HEADS)
    sls = [slice(h * HG_DK, (h + 1) * HG_DK) for h in heads]
    lb = [lb_ref[:, sl] for sl in sls]
    q = [q_ref[0, :, sl] for sl in sls]
    z = [f_ref[0, :, sl] for sl in sls]
    v = [i_ref[0, :, sl] for sl in sls]
    log_f = [jnp.log(lb[h] + (1.0 - lb[h]) * jax.nn.sigmoid(z[h])) for h in heads]
    k = [(1.0 - lb[h]) * jax.nn.sigmoid(-z[h]) for h in heads]
    b = [jnp.dot(tril, log_f[h], preferred_element_type=_F32, precision=lax.Precision.HIGHEST) for h in heads]
    b_last = [b[h][c - 1:c, :] for h in heads]
    st = [st_sc[h] for h in heads]
    v_bf = [v[h].astype(_BF) for h in heads]
    o = [lax.dot_general((q[h] * jnp.exp(b[h])).astype(_BF), st[h].astype(_BF), _NT, preferred_element_type=_F32)
         for h in heads]
    blocks = range(c // HG_SUB)
    rows = [slice(blk * HG_SUB, (blk + 1) * HG_SUB) for blk in blocks]
    a_off = {}
    for blk in blocks[1:]:
        r0 = blk * HG_SUB
        for h in heads:
            ref_row = b[h][r0 - 1:r0, :]
            qd = (q[h][rows[blk]] * jnp.exp(b[h][rows[blk]] - ref_row)).astype(_BF)
            kd = (k[h][:r0] * jnp.exp(ref_row - b[h][:r0])).astype(_BF)
            a_off[blk, h] = lax.dot_general(qd, kd, _NT, preferred_element_type=_F32)
    a_col = {}
    for blk in blocks:
        for s in range(HG_SUB):
            for h in heads:
                bb = b[h][rows[blk]]
                w = jnp.where(sub_row >= s, jnp.exp(bb - bb[s:s + 1, :]), 0.0)
                k_row = k[h][blk * HG_SUB + s:blk * HG_SUB + s + 1, :]
                a_col[blk, s, h] = jnp.sum(q[h][rows[blk]] * k_row * w, axis=-1, keepdims=True)
    off = {(blk, h): jnp.dot(a_off[blk, h].astype(_BF), v_bf[h][:blk * HG_SUB], preferred_element_type=_F32)
           for blk in blocks[1:] for h in heads}
    for h in heads:
        pieces = []
        for blk in blocks:
            ob = o[h][rows[blk]]
            if blk > 0:
                ob = ob + off[blk, h]
            for s in range(HG_SUB):
                ob = ob + a_col[blk, s, h] * v[h][blk * HG_SUB + s:blk * HG_SUB + s + 1, :]
            pieces.append(ob)
        o_ref[0, :, sls[h]] = _hg_out(jnp.concatenate(pieces, axis=0), g_ref[0, :, sls[h]], norm_row)
    st_new = []
    for h in heads:
        kd_all = (k[h] * jnp.exp(b_last[h] - b[h])).astype(_BF)
        st_new.append(st[h] * jnp.exp(b_last[h])
                      + jnp.dot(v[h].T.astype(_BF), kd_all, preferred_element_type=_F32))
        st_sc[h] = st_new[h]

    @pl.when(step == pl.num_programs(1) - 1)
    def _():
        for h in heads:
            s_ref[0, h] = st_new[h].T


def _hgrn_prompt(hg, lb, out_norm):
    bsz, seq, _ = hg.shape
    c = HG_CHUNK
    w = HG_QK_W
    blk = lambda j: pl.BlockSpec((1, c, w), lambda b, l, j=j: (b, l, j))
    return pl.pallas_call(
        _hgrn_prompt_kernel,
        grid=(bsz, seq // c),
        in_specs=[blk(0), blk(1), blk(2), blk(3),
                  pl.BlockSpec((1, w), lambda b, l: (0, 0)),
                  pl.BlockSpec((1, HG_DV), lambda b, l: (0, 0))],
        out_specs=[pl.BlockSpec((1, c, HG_V_W), lambda b, l: (b, l, 0)),
                   pl.BlockSpec((1, HG_HEADS, HG_DK, HG_DV), lambda b, l: (b, 0, 0, 0))],
        out_shape=[jax.ShapeDtypeStruct((bsz, seq, HG_V_W), _F32),
                   jax.ShapeDtypeStruct((bsz, HG_HEADS, HG_DK, HG_DV), _F32)],
        scratch_shapes=[pltpu.VMEM((HG_HEADS, HG_DV, HG_DK), _F32)],
        compiler_params=_params("parallel", "arbitrary"),
        name="hgrn_prompt",
    )(hg, hg, hg, hg, lb.reshape(1, w), out_norm.reshape(1, HG_DV))


def _hgrn_step_kernel(cols_ref, row_ref, lbc_ref, norm_ref, s_ref, o_ref, s_out_ref):
    norm_row = norm_ref[...]
    for h in range(HG_HEADS):
        sl_v = slice(2 * HG_QK_W + h * HG_DV, 2 * HG_QK_W + (h + 1) * HG_DV)
        sl_g = slice(2 * HG_QK_W + HG_V_W + h * HG_DV, 2 * HG_QK_W + HG_V_W + (h + 1) * HG_DV)
        qc = cols_ref[0, h, :, 0:1]
        zc = cols_ref[0, h, :, 1:2]
        lbc = lbc_ref[h]
        f = lbc + (1.0 - lbc) * jax.nn.sigmoid(zc)
        kc = (1.0 - lbc) * jax.nn.sigmoid(-zc)
        v_row = row_ref[0, :, sl_v]
        s_new = f * s_ref[0, h] + kc * v_row
        s_out_ref[0, h] = s_new
        o = jnp.sum(qc * s_new, axis=0, keepdims=True)
        o_ref[0, :, h * HG_DV:(h + 1) * HG_DV] = _hg_out(o, row_ref[0, :, sl_g], norm_row)


def _hgrn_step(hg, state, lb, out_norm):
    bsz = hg.shape[0]
    cols = hg[:, :2 * HG_QK_W].reshape(bsz, 2, HG_HEADS, HG_DK).transpose(0, 2, 3, 1)
    rows = hg.reshape(bsz, 1, HG_IN)
    lbc = lb.reshape(HG_HEADS, HG_DK, 1)
    st_spec = pl.BlockSpec((1, HG_HEADS, HG_DK, HG_DV), lambda b: (b, 0, 0, 0))
    o, s_new = pl.pallas_call(
        _hgrn_step_kernel,
        grid=(bsz,),
        in_specs=[pl.BlockSpec((1, HG_HEADS, HG_DK, 2), lambda b: (b, 0, 0, 0)),
                  pl.BlockSpec((1, 1, HG_IN), lambda b: (b, 0, 0)),
                  pl.BlockSpec((HG_HEADS, HG_DK, 1), lambda b: (0, 0, 0)),
                  pl.BlockSpec((1, HG_DV), lambda b: (0, 0)),
                  st_spec],
        out_specs=[pl.BlockSpec((1, 1, HG_V_W), lambda b: (b, 0, 0)), st_spec],
        out_shape=[jax.ShapeDtypeStruct((bsz, 1, HG_V_W), _F32),
                   jax.ShapeDtypeStruct(state.shape, _F32)],
        compiler_params=_params("parallel"),
        name="hgrn_step",
    )(cols, rows, lbc, out_norm.reshape(1, HG_DV), state)
    return o.reshape(bsz, HG_V_W), s_new


def _rms_rows(x, g):
    return x * lax.rsqrt(jnp.mean(x * x, axis=-1, keepdims=True) + RMS_EPS) * g


def _mla_prep_kernel(x_ref, win_ref, qn_ref, wq_ref, kvn_ref, wk_ref, cos_ref, sin_ref, hm_ref,
                     qcat_ref, kcat_ref, ckv_ref, kpe_ref):
    m = jnp.dot(x_ref[0].astype(_BF), win_ref[...], preferred_element_type=_F32)
    cq = _rms_rows(m[:, :Q_LORA], qn_ref[...])
    c_kv = _rms_rows(m[:, Q_LORA:Q_LORA + KV_LORA], kvn_ref[...])
    kr1 = m[:, Q_LORA + KV_LORA:Q_LORA + KV_LORA + LANES]
    kr2 = m[:, Q_LORA + KV_LORA + LANES:]
    cos = cos_ref[...]
    sin = sin_ref[...]
    k1 = kr1 * cos - kr2 * sin
    k2 = kr1 * sin + kr2 * cos
    ckv_ref[0] = c_kv
    lane = lax.broadcasted_iota(jnp.int32, k1.shape, 1)
    kpe_ref[0] = jnp.where(lane < ROPE_HALF, k1, k2)[:, :ROPE_DIM]
    kcat_ref[0, :, :KV_LORA] = c_kv.astype(_BF)
    kcat_ref[0, :, KV_LORA:KV_LORA + LANES] = k1.astype(_BF)
    kcat_ref[0, :, KV_LORA + LANES:] = k2.astype(_BF)

    qh = jnp.dot(cq.astype(_BF), wq_ref[...], preferred_element_type=_F32)
    nope_w = MLA_HEADS * NOPE_DIM
    q1 = qh[:, nope_w:nope_w + LANES]
    q2 = qh[:, nope_w + LANES:]
    r1 = q1 * cos - q2 * sin
    r2 = q1 * sin + q2 * cos
    for h in range(MLA_HEADS):
        q_nope = qh[:, h * NOPE_DIM:(h + 1) * NOPE_DIM].astype(_BF)
        q_abs = jnp.dot(q_nope, wk_ref[h], preferred_element_type=_F32)
        hm = hm_ref[h:h + 1, :]
        qcat_ref[0, h, :, :KV_LORA] = q_abs.astype(_BF)
        qcat_ref[0, h, :, KV_LORA:KV_LORA + LANES] = (r1 * hm).astype(_BF)
        qcat_ref[0, h, :, KV_LORA + LANES:] = (r2 * hm).astype(_BF)


def _mla_prep(x, w_in, q_norm, w_q, kv_norm, w_k, cos, sin, head_mask, tm=256):
    bsz, seq, d = x.shape
    tm = min(tm, seq)
    full = lambda a: pl.BlockSpec(a.shape, lambda b, l: (0,) * a.ndim)
    qn = q_norm.reshape(1, Q_LORA)
    kvn = kv_norm.reshape(1, KV_LORA)
    return pl.pallas_call(
        _mla_prep_kernel,
        grid=(bsz, seq // tm),
        in_specs=[pl.BlockSpec((1, tm, d), lambda b, l: (b, l, 0)),
                  full(w_in), full(qn), full(w_q), full(kvn), full(w_k),
                  pl.BlockSpec((tm, LANES), lambda b, l: (l, 0)),
                  pl.BlockSpec((tm, LANES), lambda b, l: (l, 0)),
                  full(head_mask)],
        out_specs=[pl.BlockSpec((1, MLA_HEADS, tm, MLA_KW), lambda b, l: (b, 0, l, 0)),
                   pl.BlockSpec((1, tm, MLA_KW), lambda b, l: (b, l, 0)),
                   pl.BlockSpec((1, tm, KV_LORA), lambda b, l: (b, l, 0)),
                   pl.BlockSpec((1, tm, ROPE_DIM), lambda b, l: (b, l, 0))],
        out_shape=[jax.ShapeDtypeStruct((bsz, MLA_HEADS, seq, MLA_KW), _BF),
                   jax.ShapeDtypeStruct((bsz, seq, MLA_KW), _BF),
                   jax.ShapeDtypeStruct((bsz, seq, KV_LORA), _F32),
                   jax.ShapeDtypeStruct((bsz, seq, ROPE_DIM), _F32)],
        compiler_params=_params("parallel", "parallel"),
        name="mla_prep",
    )(x, w_in, qn, w_q, kvn, w_k, cos, sin, head_mask)


def _mla_attn_kernel(q_ref, k_ref, wv_ref, o_ref, m_sc, l_sc, acc_sc, *, tq, tk):
    qi = pl.program_id(1)
    kj = pl.program_id(2)
    rows = MLA_HEADS * tq

    @pl.when(kj == 0)
    def _():
        m_sc[...] = jnp.full_like(m_sc, _NEG_INF)
        l_sc[...] = jnp.zeros_like(l_sc)
        acc_sc[...] = jnp.zeros_like(acc_sc)

    def update(masked):
        q = q_ref[0].reshape(rows, MLA_KW)
        k = k_ref[0]
        s = lax.dot_general(q, k, _NT, preferred_element_type=_F32) * MLA_SCALE
        if masked:
            q_pos = qi * tq + (lax.broadcasted_iota(jnp.int32, (rows, tk), 0) & (tq - 1))
            k_pos = kj * tk + lax.broadcasted_iota(jnp.int32, (rows, tk), 1)
            s = jnp.where(k_pos <= q_pos, s, _NEG_INF)
        m_old = m_sc[...]
        m_new = jnp.maximum(m_old, jnp.max(s, axis=-1, keepdims=True))
        alpha = jnp.exp(m_old - m_new)
        p = jnp.exp(s - m_new)
        l_sc[...] = alpha * l_sc[...] + jnp.sum(p, axis=-1, keepdims=True)
        acc_sc[...] = alpha * acc_sc[...] + jnp.dot(p.astype(_BF), k[:, :KV_LORA], preferred_element_type=_F32)
        m_sc[...] = m_new

    last_key = kj * tk + (tk - 1)

    @pl.when(last_key <= qi * tq)
    def _():
        update(masked=False)

    @pl.when((last_key > qi * tq) & (kj * tk <= qi * tq + (tq - 1)))
    def _():
        update(masked=True)

    @pl.when(kj == pl.num_programs(2) - 1)
    def _():
        o_lat = acc_sc[...] / l_sc[...]
        for h in range(MLA_HEADS):
            oh = o_lat[h * tq:(h + 1) * tq].astype(_BF)
            o_ref[0, :, h * V_DIM:(h + 1) * V_DIM] = jnp.dot(oh, wv_ref[h], preferred_element_type=_F32)


def _mla_prompt_attn(qcat, kcat, w_v, tq=256, tk=512):
    bsz, _, seq, _ = qcat.shape
    tq = min(tq, seq)
    tk = min(tk, seq)
    assert tq & (tq - 1) == 0, "query tile must be a power of two (row -> position uses a bit mask)"

    def k_index(b, i, j):
        return (b, jnp.minimum(j, (i * tq + tq - 1) // tk), 0)

    return pl.pallas_call(
        functools.partial(_mla_attn_kernel, tq=tq, tk=tk),
        grid=(bsz, seq // tq, seq // tk),
        in_specs=[pl.BlockSpec((1, MLA_HEADS, tq, MLA_KW), lambda b, i, j: (b, 0, i, 0)),
                  pl.BlockSpec((1, tk, MLA_KW), k_index),
                  pl.BlockSpec(w_v.shape, lambda b, i, j: (0, 0, 0))],
        out_specs=pl.BlockSpec((1, tq, MLA_HEADS * V_DIM), lambda b, i, j: (b, i, 0)),
        out_shape=jax.ShapeDtypeStruct((bsz, seq, MLA_HEADS * V_DIM), _F32),
        scratch_shapes=[pltpu.VMEM((MLA_HEADS * tq, 1), _F32),
                        pltpu.VMEM((MLA_HEADS * tq, 1), _F32),
                        pltpu.VMEM((MLA_HEADS * tq, KV_LORA), _F32)],
        compiler_params=_params("parallel", "parallel", "arbitrary"),
        name="mla_prompt_attn",
    )(qcat, kcat, w_v)


def _mla_decode_kernel(pt_ref, q_ref, knew_ref, wv_ref, fold_ref, *refs, pages):
    lat_refs = refs[:pages]
    kr_refs = refs[pages:2 * pages]
    o_ref = refs[2 * pages]
    m_sc, l_sc, acc_sc = refs[2 * pages + 1:]
    g = pl.program_id(1)

    @pl.when(g == 0)
    def _():
        m_sc[...] = jnp.full_like(m_sc, _NEG_INF)
        l_sc[...] = jnp.zeros_like(l_sc)
        acc_sc[...] = jnp.zeros_like(acc_sc)

    q = q_ref[0]
    q_abs = q[:, :KV_LORA]
    q_pe = jnp.dot(q[:, KV_LORA:], fold_ref[...], preferred_element_type=_F32).astype(_BF)

    lat = [lat_refs[p][0, 0].astype(_BF) for p in range(pages)]
    s = jnp.concatenate(
        [lax.dot_general(q_abs, lat[p], _NT, preferred_element_type=_F32)
         + jnp.dot(q_pe, kr_refs[p][0].astype(_BF), preferred_element_type=_F32)
         for p in range(pages)], axis=1) * MLA_SCALE
    m_old = m_sc[...]
    m_run = jnp.maximum(m_old, jnp.max(s, axis=-1, keepdims=True))
    alpha = jnp.exp(m_old - m_run)
    pr = jnp.exp(s - m_run)
    l_run = alpha * l_sc[...] + jnp.sum(pr, axis=-1, keepdims=True)
    pr = pr.astype(_BF)
    pv = functools.reduce(jnp.add, [jnp.dot(pr[:, p * PAGE_SIZE:(p + 1) * PAGE_SIZE], lat[p],
                                            preferred_element_type=_F32) for p in range(pages)])
    acc = alpha * acc_sc[...] + pv
    m_sc[...], l_sc[...], acc_sc[...] = m_run, l_run, acc

    @pl.when(g == pl.num_programs(1) - 1)
    def _():
        k_new = knew_ref[0].astype(_F32)
        s_self = jnp.sum(q.astype(_F32) * k_new, axis=-1, keepdims=True) * MLA_SCALE
        m_fin = jnp.maximum(m_run, s_self)
        alpha = jnp.exp(m_run - m_fin)
        p_self = jnp.exp(s_self - m_fin)
        l_fin = alpha * l_run + p_self
        c_new = k_new[:, :KV_LORA]
        o_lat = (alpha * acc + p_self.astype(_BF).astype(_F32) * c_new) / l_fin
        for h in range(MLA_HEADS):
            o_ref[0, :, h * V_DIM:(h + 1) * V_DIM] = jnp.dot(
                o_lat[h:h + 1].astype(_BF), wv_ref[h], preferred_element_type=_F32)


def _mla_decode_attn(qcat, kcat, cache_lat, cache_kr, page_table, w_v, pages=16):
    bsz = qcat.shape[0]
    n_pages = page_table.shape[1]
    lat4 = cache_lat.reshape((1,) + cache_lat.shape)
    kr3 = jnp.swapaxes(cache_kr, 1, 2)

    def lat_spec(p):
        return pl.BlockSpec((1, 1, PAGE_SIZE, KV_LORA), lambda b, g, pt: (0, pt[b, g * pages + p], 0, 0))

    def kr_spec(p):
        return pl.BlockSpec((1, ROPE_DIM, PAGE_SIZE), lambda b, g, pt: (pt[b, g * pages + p], 0, 0))

    lane = jnp.arange(2 * LANES)
    dst = (lane // LANES) * ROPE_HALF + lane % ROPE_HALF
    fold = (jnp.arange(ROPE_DIM)[None, :] == dst[:, None]).astype(_BF)
    in_specs = [pl.BlockSpec((1, MLA_HEADS, MLA_KW), lambda b, g, pt: (b, 0, 0)),
                pl.BlockSpec((1, 1, MLA_KW), lambda b, g, pt: (b, 0, 0)),
                pl.BlockSpec(w_v.shape, lambda b, g, pt: (0, 0, 0)),
                pl.BlockSpec(fold.shape, lambda b, g, pt: (0, 0))]
    in_specs += [lat_spec(p) for p in range(pages)]
    in_specs += [kr_spec(p) for p in range(pages)]
    out = pl.pallas_call(
        functools.partial(_mla_decode_kernel, pages=pages),
        grid_spec=pltpu.PrefetchScalarGridSpec(
            num_scalar_prefetch=1,
            grid=(bsz, n_pages // pages),
            in_specs=in_specs,
            out_specs=pl.BlockSpec((1, 1, MLA_HEADS * V_DIM), lambda b, g, pt: (b, 0, 0)),
            scratch_shapes=[pltpu.VMEM((MLA_HEADS, 1), _F32),
                            pltpu.VMEM((MLA_HEADS, 1), _F32),
                            pltpu.VMEM((MLA_HEADS, KV_LORA), _F32)]),
        out_shape=jax.ShapeDtypeStruct((bsz, 1, MLA_HEADS * V_DIM), _F32),
        compiler_params=_params("parallel", "arbitrary"),
        name="mla_decode_attn",
    )(page_table, qcat, kcat, w_v, fold, *([lat4] * pages), *([kr3] * pages))
    return out.reshape(bsz, MLA_HEADS * V_DIM)


_CONV_HALO = 32
_CONV_ROWS = 32


def _conv_prompt_kernel(u_ref, wdw_ref, bdw_ref, g_ref, b_ref, o_ref, ext_sc, sh_sc, *, tl):
    @pl.when(pl.program_id(1) == 0)
    def _():
        ext_sc[:_CONV_HALO, :] = jnp.zeros((_CONV_HALO, CONV_DIM), _F32)

    ext_sc[_CONV_HALO:, :] = u_ref[0]
    first = _CONV_HALO - (CONV_WIDTH - 1)
    span = sh_sc.shape[1]
    for r in range(1, SUBLANES):
        sh_sc[r - 1] = ext_sc[r:r + span, :]
    bias, gain, shift = bdw_ref[...], g_ref[...], b_ref[...]
    for c0 in range(0, tl, _CONV_ROWS):
        y = jnp.zeros((_CONV_ROWS, CONV_DIM), _F32) + bias
        for w in range(CONV_WIDTH):
            r = (first + w) % SUBLANES
            a = first + w - r + c0
            src = ext_sc[a:a + _CONV_ROWS, :] if r == 0 else sh_sc[r - 1, a:a + _CONV_ROWS, :]
            y = y + src * wdw_ref[w:w + 1, :]
        y = _layer_norm_rows(y, gain, shift)
        o_ref[0, c0:c0 + _CONV_ROWS, :] = y * jax.nn.sigmoid(y)
    ext_sc[:_CONV_HALO, :] = ext_sc[tl:tl + _CONV_HALO, :]


def _conv_prompt(u, w_dw, b_dw, ln_g, ln_b, tl=256):
    bsz, seq, d = u.shape
    tl = min(tl, seq)
    vec = pl.BlockSpec((1, d), lambda b, l: (0, 0))
    return pl.pallas_call(
        functools.partial(_conv_prompt_kernel, tl=tl),
        grid=(bsz, seq // tl),
        in_specs=[pl.BlockSpec((1, tl, d), lambda b, l: (b, l, 0)),
                  pl.BlockSpec((CONV_WIDTH, d), lambda b, l: (0, 0)), vec, vec, vec],
        out_specs=pl.BlockSpec((1, tl, d), lambda b, l: (b, l, 0)),
        out_shape=jax.ShapeDtypeStruct((bsz, seq, d), _F32),
        scratch_shapes=[pltpu.VMEM((_CONV_HALO + tl, d), _F32),
                        pltpu.VMEM((SUBLANES - 1, tl + _CONV_HALO - SUBLANES, d), _F32)],
        compiler_params=_params("parallel", "arbitrary"),
        name="conv_prompt",
    )(u, w_dw, b_dw.reshape(1, d), ln_g.reshape(1, d), ln_b.reshape(1, d))


def _conv_step_kernel(u_ref, st_ref, wdw_ref, bdw_ref, g_ref, b_ref, o_ref, st_out_ref):
    st = st_ref[0]
    u = u_ref[0]
    hist = CONV_WIDTH - 1
    y = (jnp.sum(st * wdw_ref[:hist, :], axis=0, keepdims=True) + u * wdw_ref[hist:CONV_WIDTH, :]
         + bdw_ref[...])
    y = _layer_norm_rows(y, g_ref[...], b_ref[...])
    o_ref[0] = y * jax.nn.sigmoid(y)
    st_out_ref[0, :hist - 1, :] = st[1:, :]
    st_out_ref[0, hist - 1:, :] = u


def _conv_step(u, state, w_dw, b_dw, ln_g, ln_b):
    bsz, d = u.shape
    hist = CONV_WIDTH - 1
    vec = pl.BlockSpec((1, d), lambda b: (0, 0))
    y, st_new = pl.pallas_call(
        _conv_step_kernel,
        grid=(bsz,),
        in_specs=[pl.BlockSpec((1, 1, d), lambda b: (b, 0, 0)),
                  pl.BlockSpec((1, hist, d), lambda b: (b, 0, 0)),
                  pl.BlockSpec((CONV_WIDTH, d), lambda b: (0, 0)), vec, vec, vec],
        out_specs=[pl.BlockSpec((1, 1, d), lambda b: (b, 0, 0)),
                   pl.BlockSpec((1, hist, d), lambda b: (b, 0, 0))],
        out_shape=[jax.ShapeDtypeStruct((bsz, 1, d), _F32),
                   jax.ShapeDtypeStruct((bsz, hist, d), _F32)],
        compiler_params=_params("parallel"),
        name="conv_step",
    )(u.reshape(bsz, 1, d), state, w_dw, b_dw.reshape(1, d), ln_g.reshape(1, d), ln_b.reshape(1, d))
    return y.reshape(bsz, d), st_new


def _oddeven_merge(lo, hi, r):
    step = r * 2
    if step < hi - lo:
        yield from _oddeven_merge(lo, hi, step)
        yield from _oddeven_merge(lo + r, hi, step)
        yield from [(i, i + r) for i in range(lo + r, hi - r, step)]
    else:
        yield (lo, lo + r)


def _oddeven_sort(lo, hi):
    if hi - lo >= 1:
        mid = lo + (hi - lo) // 2
        yield from _oddeven_sort(lo, mid)
        yield from _oddeven_sort(mid + 1, hi)
        yield from _oddeven_merge(lo, hi, 1)


_SORT_NET = tuple(_oddeven_sort(0, PEER_TOPK - 1))
_MERGE_NET = tuple((i, i + d) for d in (8, 4, 2, 1) for i in range(PEER_TOPK) if i & d == 0)


def _exchange(x, net):
    for i, j in net:
        x[i], x[j] = jnp.maximum(x[i], x[j]), jnp.minimum(x[i], x[j])


def _merge_sublanes(x):
    for shift in (4, 2, 1):
        x = [jnp.maximum(x[i], pltpu.roll(x[PEER_TOPK - 1 - i], shift, 0)) for i in range(PEER_TOPK)]
        _exchange(x, _MERGE_NET)
    return x


def _top16(s):
    x = [s[SUBLANES * v:SUBLANES * (v + 1), :] for v in range(PEER_TOPK)]
    _exchange(x, _SORT_NET)
    return _merge_sublanes(x)


def _next_value(s, last):
    n_ge = jnp.sum((s >= last).astype(_F32), axis=0, keepdims=True)
    below = jnp.max(jnp.where(s < last, s, _NEG_INF), axis=0, keepdims=True)
    return jnp.where(n_ge > PEER_TOPK, last, below)


def _peer_route_kernel(x_ref, wq_ref, keys_ref, th_ref, ea_ref, eb_ref):
    xb = x_ref[...].astype(_BF)
    q_t = lax.dot_general(wq_ref[...], xb, _NT, preferred_element_type=_F32)
    tm = xb.shape[0]
    sub = lax.broadcasted_iota(jnp.int32, (SUBLANES, tm), 0)
    for h in range(PEER_HEADS):
        scores, tops = [], []
        for p in range(2):
            r0 = (h * 2 + p) * PEER_HALF
            q_hp = q_t[r0:r0 + PEER_HALF, :].astype(_BF)
            s = jnp.dot(keys_ref[p, h], q_hp, preferred_element_type=_F32)
            scores.append(s)
            tops.append(_top16(s))
        s1, s2 = scores
        y1, y2 = tops
        v1_lo, v1_hi = y1[0], y1[SUBLANES]
        for a in range(1, SUBLANES):
            v1_lo = jnp.where(sub == a, y1[a], v1_lo)
            v1_hi = jnp.where(sub == a, y1[SUBLANES + a], v1_hi)
        cells = []
        for b in range(PEER_TOPK):
            c = v1_lo + y2[b]
            lim = PEER_TOPK // (b + 1)
            cells.append(c if lim >= SUBLANES else jnp.where(sub < lim, c, _NEG_INF))
        extra = v1_hi + y2[0]
        x = list(cells)
        x[PEER_TOPK - 1] = jnp.maximum(x[PEER_TOPK - 1], extra)
        _exchange(x, _MERGE_NET)
        tau = _merge_sublanes(x)[PEER_TOPK - 1]
        m1, m2 = y1[0], y2[0]
        e1_lo = jnp.exp(v1_lo - m1)
        z = jnp.where(extra >= tau, jnp.exp(v1_hi - m1), 0.0)
        for b in range(PEER_TOPK):
            z = z + jnp.where(cells[b] >= tau, e1_lo * jnp.exp(y2[b] - m2), 0.0)
        z = jnp.sum(z, axis=0, keepdims=True)
        ea = jnp.exp(s1 - m1[0:1, :]) / z
        eb = jnp.exp(s2 - m2[0:1, :])
        tau_row = tau[0:1, :]
        n_ge = functools.reduce(jnp.add, [(c >= tau).astype(_F32) for c in cells + [extra]])
        below = functools.reduce(jnp.maximum, [jnp.where(c < tau, c, _NEG_INF) for c in cells + [extra]])
        c17 = jnp.where(jnp.sum(n_ge, axis=0, keepdims=True) > PEER_TOPK, tau_row,
                        jnp.max(below, axis=0, keepdims=True))
        c17 = jnp.maximum(c17, jnp.maximum(m1[0:1, :] + _next_value(s2, y2[PEER_TOPK - 1][0:1, :]),
                                           m2[0:1, :] + _next_value(s1, y1[PEER_TOPK - 1][0:1, :])))
        theta = jnp.exp(0.5 * (tau_row + c17) - s1 - m2[0:1, :])
        for cb in range(tm // LANES):
            cols = slice(cb * LANES, (cb + 1) * LANES)
            th_ref[h, cb] = theta[:, cols]
            ea_ref[h, cb] = ea[:, cols]
            eb_ref[h, cb] = eb[:, cols]


def _peer_expert_kernel(xt_ref, u_ref, vt_ref, th_ref, ea_ref, eb_ref, o_ref,
                        h0_sc, h1_sc, p0_sc, p1_sc, acc_sc, *, te, ne):
    g = pl.program_id(0)
    e_c = jnp.maximum(g - 2, 0) % ne

    @pl.when(g == 0)
    def _():
        for ref in (h0_sc, h1_sc, p0_sc, p1_sc):
            ref[...] = jnp.zeros_like(ref)

    @pl.when((g < 2) | (e_c == 0))
    def _():
        acc_sc[...] = jnp.zeros_like(acc_sc)

    groups = N_KEYS // SUBLANES

    def key_rows(ii, n_rows):
        return pl.ds(ii * n_rows, n_rows)

    def gate_tile(h_b, p_b, ii, cb):
        w = [jnp.zeros((SUBLANES, LANES), _F32) for _ in range(groups)]
        for h in range(PEER_HEADS):
            th = jnp.broadcast_to(th_ref[h, cb, pl.ds(ii, 1), :], (SUBLANES, LANES))
            ea = jnp.broadcast_to(ea_ref[h, cb, pl.ds(ii, 1), :], (SUBLANES, LANES))
            for j in range(groups):
                eb = eb_ref[h, cb, j * SUBLANES:(j + 1) * SUBLANES, :]
                w[j] = w[j] + jnp.where(eb >= th, ea * eb, 0.0)
            yield
        rows = key_rows(ii, N_KEYS)
        hh = h_b[cb, rows, :]
        act = 0.5 * hh * (1.0 + lax.erf(hh * (2.0 ** -0.5)))
        p_b[cb, rows, :] = (jnp.concatenate(w, axis=0) * act).astype(_BF)
        yield

    def step(h_a, h_b, p_b, p_c):
        n_cb = h_b.shape[0]
        per = min(MXU_WIDTH // LANES, n_cb)
        n_ii = te // N_KEYS
        d_rows = acc_sc.shape[0] // n_ii

        def stage_c(ii, cbs, cols):
            out_rows = key_rows(ii, d_rows)
            p_grp = jnp.concatenate([p_c[cb] for cb in cbs], axis=1)
            acc_sc[out_rows, cols] += jnp.dot(vt_ref[0, out_rows, :], p_grp, preferred_element_type=_F32)

        def stage_a(ii, cbs, cols):
            rows = key_rows(ii, N_KEYS)
            hv = jnp.dot(u_ref[rows, :], xt_ref[0, :, cols], preferred_element_type=_F32)
            for k, cb in enumerate(cbs):
                h_a[cb, rows, :] = hv[:, k * LANES:(k + 1) * LANES]

        for ii in range(n_ii):
            for grp in range(n_cb // per):
                cbs = range(grp * per, (grp + 1) * per)
                cols = slice(grp * per * LANES, (grp + 1) * per * LANES)
                pieces = [functools.partial(f, ii, cbs, cols) for f in (stage_c, stage_a)]
                chunks = itertools.chain.from_iterable(gate_tile(h_b, p_b, ii, cb) for cb in cbs)
                n_chunks = len(cbs) * (PEER_HEADS + 1)
                every = max(n_chunks // len(pieces), 1)
                for idx in range(n_chunks):
                    if idx % every == 0 and pieces:
                        pieces.pop(0)()
                    next(chunks)
                for piece in pieces:
                    piece()

    @pl.when(g % 2 == 0)
    def _():
        step(h0_sc, h1_sc, p1_sc, p0_sc)

    @pl.when(g % 2 == 1)
    def _():
        step(h1_sc, h0_sc, p0_sc, p1_sc)

    @pl.when((g >= 2) & (e_c == ne - 1))
    def _():
        o_ref[...] = acc_sc[...].T


def _peer(x, wq_t, keys, u_bf, v_tiles, tm=512, tr=256):
    n, d = x.shape
    te = v_tiles.shape[2]
    tm = min(tm, n)
    tr = min(tr, n)
    rows = PEER_HEADS * 2 * PEER_HALF
    sc_spec = pl.BlockSpec((PEER_HEADS, tr // LANES, N_KEYS, LANES), lambda t: (0, t, 0, 0))
    sc_shape = jax.ShapeDtypeStruct((PEER_HEADS, n // LANES, N_KEYS, LANES), _F32)
    th, ea, eb = pl.pallas_call(
        _peer_route_kernel,
        grid=(n // tr,),
        in_specs=[pl.BlockSpec((tr, d), lambda t: (t, 0)),
                  pl.BlockSpec((rows, d), lambda t: (0, 0)),
                  pl.BlockSpec(keys.shape, lambda t: (0, 0, 0, 0))],
        out_specs=[sc_spec, sc_spec, sc_spec],
        out_shape=[sc_shape, sc_shape, sc_shape],
        compiler_params=_params("parallel"),
        name="peer_route",
    )(x, wq_t, keys)

    nt = n // tm
    ne = N_EXPERTS // te
    last = nt * ne - 1
    step_a = lambda g: jnp.minimum(g, last)
    tile_b = lambda g: jnp.minimum(jnp.maximum(g - 1, 0) // ne, nt - 1)
    step_c = lambda g: jnp.maximum(g - 2, 0)
    assert (te // N_KEYS) % SUBLANES == 0, "first-half key rows of an expert tile must fill whole sublane tiles"
    n_cb = tm // LANES
    xt_tiles = x.astype(_BF).reshape(nt, tm, d).transpose(0, 2, 1)
    sc_all = pl.BlockSpec((PEER_HEADS, n_cb, N_KEYS, LANES), lambda g: (0, tile_b(g), 0, 0))
    sc_rows = pl.BlockSpec((PEER_HEADS, n_cb, te // N_KEYS, LANES),
                           lambda g: (0, tile_b(g), jnp.maximum(g - 1, 0) % ne, 0))
    return pl.pallas_call(
        functools.partial(_peer_expert_kernel, te=te, ne=ne),
        grid=(nt * ne + 2,),
        in_specs=[pl.BlockSpec((1, d, tm), lambda g: (step_a(g) // ne, 0, 0)),
                  pl.BlockSpec((te, d), lambda g: (step_a(g) % ne, 0)),
                  pl.BlockSpec((1, d, te), lambda g: (step_c(g) % ne, 0, 0)),
                  sc_rows, sc_rows, sc_all],
        out_specs=pl.BlockSpec((tm, d), lambda g: (step_c(g) // ne, 0)),
        out_shape=jax.ShapeDtypeStruct((n, d), _F32),
        scratch_shapes=[pltpu.VMEM((n_cb, te, LANES), _F32), pltpu.VMEM((n_cb, te, LANES), _F32),
                        pltpu.VMEM((n_cb, te, LANES), _BF), pltpu.VMEM((n_cb, te, LANES), _BF),
                        pltpu.VMEM((d, tm), _F32)],
        compiler_params=_params("arbitrary"),
        name="peer_experts",
    )(xt_tiles, u_bf, v_tiles, th, ea, eb)


def _rope_tables(pos):
    inv_freq = ROPE_THETA ** (-jnp.arange(ROPE_HALF, dtype=_F32) / ROPE_HALF)
    ang = pos.astype(_F32)[:, None] * inv_freq[None, :]
    return jnp.tile(jnp.cos(ang), (1, MLA_HEADS)), jnp.tile(jnp.sin(ang), (1, MLA_HEADS))


def _mixer_weights(w_in, w_uq, w_ukv, w_out):
    w_hg = w_in[:, :HG_IN].astype(_BF)
    o = HG_IN
    cq_w = w_in[:, o:o + Q_LORA]
    ckv_w = w_in[:, o + Q_LORA:o + Q_LORA + KV_LORA]
    kr_w = w_in[:, o + Q_LORA + KV_LORA:]
    w_mla = jnp.concatenate([cq_w, ckv_w, jnp.tile(kr_w[:, :ROPE_HALF], (1, MLA_HEADS)),
                             jnp.tile(kr_w[:, ROPE_HALF:], (1, MLA_HEADS))], axis=1).astype(_BF)
    wq = w_uq.reshape(Q_LORA, MLA_HEADS, NOPE_DIM + ROPE_DIM)
    w_q = jnp.concatenate([wq[:, :, :NOPE_DIM].reshape(Q_LORA, -1),
                           wq[:, :, NOPE_DIM:NOPE_DIM + ROPE_HALF].reshape(Q_LORA, -1),
                           wq[:, :, NOPE_DIM + ROPE_HALF:].reshape(Q_LORA, -1)], axis=1).astype(_BF)
    w_k = jnp.transpose(w_ukv[:, :, :NOPE_DIM], (1, 2, 0)).astype(_BF)
    w_v = jnp.transpose(w_ukv[:, :, NOPE_DIM:], (1, 0, 2)).astype(_BF)
    return w_hg, w_mla, w_q, w_k, w_v, w_out[:HG_V_W].astype(_BF), w_out[HG_V_W:].astype(_BF)


def kernel(x_prompt, x_sample, cache_mla_latent, cache_mla_krope, state_hgrn, state_conv, page_table, hg_lb_logits, mix_w_in, mla_q_norm, mla_w_uq, mla_kv_norm, mla_w_ukv, hg_out_norm, mix_w_out, conv_w_pw1, conv_b_pw1, conv_w_dw, conv_b_dw, conv_ln_g, conv_ln_b, conv_w_pw2, conv_b_pw2, peer_w_q, peer_sub_keys, peer_u, peer_v, ln_g, ln_b):
    bp, lp, d = x_prompt.shape
    bs, ls, _ = x_sample.shape
    assert ls == 1, "the sample group decodes one token per sequence"
    past_len = page_table.shape[1] * PAGE_SIZE
    lb_all = jnp.cumsum(jax.nn.softmax(hg_lb_logits.astype(_F32), axis=0), axis=0)
    head_mask = (jnp.arange(LANES)[None, :] // ROPE_HALF == jnp.arange(MLA_HEADS)[:, None]).astype(_F32)
    cos_p, sin_p = _rope_tables(jnp.arange(lp))
    cos_s, sin_s = _rope_tables(jnp.full((bs,), past_len))

    hp = x_prompt.reshape(bp * lp, d)
    hs = x_sample.reshape(bs, d)
    outs_p = {"lat": [], "kpe": [], "hg": [], "conv": []}
    outs_s = {"lat": [], "kpe": [], "hg": [], "conv": []}
    for layer in range(DEPTH):
        if layer % 2 == 0:
            e = layer // 2
            lb = lb_all[layer]
            w_hg, w_mla, w_q, w_k, w_v, w_out_hg, w_out_mla = _mixer_weights(
                mix_w_in[e], mla_w_uq[e], mla_w_ukv[e], mix_w_out[e])
            hg = _linear([hp], [w_hg]).reshape(bp, lp, HG_IN)
            o_hg, s_new = _hgrn_prompt(hg, lb, hg_out_norm[e])
            qcat, kcat, c_kv, k_pe = _mla_prep(hp.reshape(bp, lp, d), w_mla, mla_q_norm[e], w_q, mla_kv_norm[e],
                                               w_k, cos_p, sin_p, head_mask)
            o_mla = _mla_prompt_attn(qcat, kcat, w_v)
            mp = _linear([o_hg.reshape(bp * lp, HG_V_W), o_mla.reshape(bp * lp, MLA_HEADS * V_DIM)],
                         [w_out_hg, w_out_mla])
            outs_p["lat"].append(c_kv)
            outs_p["kpe"].append(k_pe)
            outs_p["hg"].append(s_new)
            hg = _linear([hs], [w_hg])
            o_hg, s_new = _hgrn_step(hg, state_hgrn[e], lb, hg_out_norm[e])
            qcat, kcat, c_kv, k_pe = _mla_prep(hs.reshape(1, bs, d), w_mla, mla_q_norm[e], w_q, mla_kv_norm[e],
                                               w_k, cos_s, sin_s, head_mask)
            o_mla = _mla_decode_attn(jnp.transpose(qcat[0], (1, 0, 2)), kcat.reshape(bs, 1, MLA_KW),
                                     cache_mla_latent[e], cache_mla_krope[e], page_table, w_v)
            ms = _linear([o_hg, o_mla], [w_out_hg, w_out_mla])
            outs_s["lat"].append(c_kv.reshape(bs, 1, KV_LORA))
            outs_s["kpe"].append(k_pe.reshape(bs, 1, ROPE_DIM))
            outs_s["hg"].append(s_new)
        else:
            o = layer // 2
            w1 = conv_w_pw1[o].astype(_BF)
            w2 = conv_w_pw2[o].astype(_BF)
            u = _glu_linear(hp, w1, conv_b_pw1[o]).reshape(bp, lp, CONV_DIM)
            y = _conv_prompt(u, conv_w_dw[o], conv_b_dw[o], conv_ln_g[o], conv_ln_b[o])
            mp = _linear([y.reshape(bp * lp, CONV_DIM)], [w2], conv_b_pw2[o])
            outs_p["conv"].append(u[:, lp - (CONV_WIDTH - 1):, :])
            u = _glu_linear(hs, w1, conv_b_pw1[o])
            y, st_new = _conv_step(u, state_conv[o], conv_w_dw[o], conv_b_dw[o], conv_ln_g[o], conv_ln_b[o])
            ms = _linear([y], [w2], conv_b_pw2[o])
            outs_s["conv"].append(st_new)
        hp = _add_ln(hp, mp, ln_g[layer, 0], ln_b[layer, 0])
        hs = _add_ln(hs, ms, ln_g[layer, 0], ln_b[layer, 0])
        wq_t = peer_w_q[layer].T.astype(_BF)
        keys = peer_sub_keys[layer].astype(_BF)
        u_bf = peer_u[layer].astype(_BF)
        v_tiles = peer_v[layer].astype(_BF).reshape(N_EXPERTS // PEER_TE, PEER_TE, d).transpose(0, 2, 1)
        hp = _add_ln(hp, _peer(hp, wq_t, keys, u_bf, v_tiles), ln_g[layer, 1], ln_b[layer, 1])
        hs = _add_ln(hs, _peer(hs, wq_t, keys, u_bf, v_tiles), ln_g[layer, 1], ln_b[layer, 1])
    stack = lambda xs: jnp.stack(xs)
    return (hp.reshape(bp, lp, d), hs.reshape(bs, ls, d),
            stack(outs_p["lat"]), stack(outs_p["kpe"]), stack(outs_p["hg"]), stack(outs_p["conv"]),
            stack(outs_s["lat"]), stack(outs_s["kpe"]), stack(outs_s["hg"]), stack(outs_s["conv"]))
```

```python
import functools
import itertools
import math

import jax
import jax.numpy as jnp
from jax import lax
from jax.experimental import pallas as pl
from jax.experimental.pallas import tpu as pltpu

D_MODEL = 1024
DEPTH = 2
PAGE_SIZE = 128
HG_HEADS = 4
HG_DK = 128
HG_DV = 128
HG_CHUNK = 64
HG_SUB = 8
MLA_HEADS = 4
Q_LORA = 384
KV_LORA = 256
NOPE_DIM = 128
ROPE_DIM = 64
ROPE_HALF = ROPE_DIM // 2
V_DIM = 128
ROPE_THETA = 10000.0
MLA_SCALE = (NOPE_DIM + ROPE_DIM) ** -0.5
CONV_DIM = D_MODEL
CONV_WIDTH = 31
PEER_HEADS = 8
N_KEYS = 128
N_EXPERTS = N_KEYS * N_KEYS
PEER_HALF = 128
PEER_TOPK = 16
PEER_TE = 1024
GATE_KEYS = 1
DN_ALPHA = (2 * DEPTH) ** 0.25
LN_EPS = 1e-5
RMS_EPS = 1e-6
HG_QK_W = HG_HEADS * HG_DK
HG_V_W = HG_HEADS * HG_DV
HG_IN = 2 * HG_QK_W + 2 * HG_V_W
MLA_KW = KV_LORA + 2 * 128

LANES = 128
SUBLANES = 8
MXU_WIDTH = 256
VMEM_LIMIT = 56 * 1024 * 1024

_BF = jnp.bfloat16
_F32 = jnp.float32
_NT = (((1,), (1,)), ((), ()))
_TN = (((0,), (0,)), ((), ()))
_NEG_INF = float("-inf")


def _params(*sem):
    return pltpu.CompilerParams(dimension_semantics=sem, vmem_limit_bytes=VMEM_LIMIT)


def _linear_kernel(*refs, n_in, has_bias):
    o_ref = refs[-1]
    acc = None
    for x_ref, w_ref in zip(refs[:n_in], refs[n_in:2 * n_in]):
        d = jnp.dot(x_ref[...].astype(_BF), w_ref[...], preferred_element_type=_F32)
        acc = d if acc is None else acc + d
    if has_bias:
        acc = acc + refs[2 * n_in][...]
    o_ref[...] = acc


def _linear(xs, ws, bias=None, tm=512):
    m = xs[0].shape[0]
    n = ws[0].shape[1]
    tm = min(tm, m)
    in_specs = [pl.BlockSpec((tm, x.shape[1]), lambda i: (i, 0)) for x in xs]
    in_specs += [pl.BlockSpec(w.shape, lambda i: (0, 0)) for w in ws]
    args = list(xs) + list(ws)
    if bias is not None:
        in_specs.append(pl.BlockSpec((1, n), lambda i: (0, 0)))
        args.append(bias.reshape(1, n))
    return pl.pallas_call(
        functools.partial(_linear_kernel, n_in=len(xs), has_bias=bias is not None),
        grid=(m // tm,),
        in_specs=in_specs,
        out_specs=pl.BlockSpec((tm, n), lambda i: (i, 0)),
        out_shape=jax.ShapeDtypeStruct((m, n), _F32),
        compiler_params=_params("parallel"),
        name="linear",
    )(*args)


def _linear_ln_kernel(*refs, n_in, has_bias):
    o_ref = refs[-1]
    acc = None
    for x_ref, w_ref in zip(refs[:n_in], refs[n_in:2 * n_in]):
        d = jnp.dot(x_ref[...].astype(_BF), w_ref[...], preferred_element_type=_F32)
        acc = d if acc is None else acc + d
    k = 2 * n_in
    if has_bias:
        acc = acc + refs[k][...]
        k += 1
    r_ref, g_ref, b_ref = refs[k:k + 3]
    o_ref[...] = _layer_norm_rows(DN_ALPHA * r_ref[...] + acc, g_ref[...], b_ref[...])


def _linear_ln(xs, ws, bias, resid, g, b, tm=512):
    m, n = resid.shape
    tm = min(tm, m)
    row = pl.BlockSpec((tm, n), lambda i: (i, 0))
    vec = pl.BlockSpec((1, n), lambda i: (0, 0))
    in_specs = [pl.BlockSpec((tm, x.shape[1]), lambda i: (i, 0)) for x in xs]
    in_specs += [pl.BlockSpec(w.shape, lambda i: (0, 0)) for w in ws]
    args = list(xs) + list(ws)
    if bias is not None:
        in_specs.append(vec)
        args.append(bias.reshape(1, n))
    in_specs += [row, vec, vec]
    args += [resid, g.reshape(1, n), b.reshape(1, n)]
    return pl.pallas_call(
        functools.partial(_linear_ln_kernel, n_in=len(xs), has_bias=bias is not None),
        grid=(m // tm,),
        in_specs=in_specs,
        out_specs=row,
        out_shape=jax.ShapeDtypeStruct((m, n), _F32),
        compiler_params=_params("parallel"),
        name="linear_ln",
    )(*args)


def _glu_kernel(x_ref, wa_ref, wb_ref, ba_ref, bb_ref, o_ref):
    xb = x_ref[...].astype(_BF)
    a = jnp.dot(xb, wa_ref[...], preferred_element_type=_F32) + ba_ref[...]
    g = jnp.dot(xb, wb_ref[...], preferred_element_type=_F32) + bb_ref[...]
    o_ref[...] = a * jax.nn.sigmoid(g)


def _glu_linear(x, w, b, tm=512):
    m, k = x.shape
    n = w.shape[1] // 2
    tm = min(tm, m)
    b2 = b.reshape(1, 2 * n)
    return pl.pallas_call(
        _glu_kernel,
        grid=(m // tm,),
        in_specs=[pl.BlockSpec((tm, k), lambda i: (i, 0)),
                  pl.BlockSpec((k, n), lambda i: (0, 0)),
                  pl.BlockSpec((k, n), lambda i: (0, 1)),
                  pl.BlockSpec((1, n), lambda i: (0, 0)),
                  pl.BlockSpec((1, n), lambda i: (0, 1))],
        out_specs=pl.BlockSpec((tm, n), lambda i: (i, 0)),
        out_shape=jax.ShapeDtypeStruct((m, n), _F32),
        compiler_params=_params("parallel"),
        name="glu_linear",
    )(x, w, w, b2, b2)


def _layer_norm_rows(v, g, b):
    mu = jnp.mean(v, axis=-1, keepdims=True)
    c = v - mu
    var = jnp.mean(c * c, axis=-1, keepdims=True)
    return c * lax.rsqrt(var + LN_EPS) * g + b


def _add_ln_kernel(x_ref, f_ref, g_ref, b_ref, o_ref):
    o_ref[...] = _layer_norm_rows(DN_ALPHA * x_ref[...] + f_ref[...], g_ref[...], b_ref[...])


def _add_ln(x, f, g, b, tm=512):
    m, d = x.shape
    tm = min(tm, m)
    row = pl.BlockSpec((tm, d), lambda i: (i, 0))
    vec = pl.BlockSpec((1, d), lambda i: (0, 0))
    return pl.pallas_call(
        _add_ln_kernel,
        grid=(m // tm,),
        in_specs=[row, row, vec, vec],
        out_specs=row,
        out_shape=jax.ShapeDtypeStruct((m, d), _F32),
        compiler_params=_params("parallel"),
        name="add_ln",
    )(x, f, g.reshape(1, d), b.reshape(1, d))


def _hg_out(o, gate, norm_row):
    y = o * lax.rsqrt(jnp.mean(o * o, axis=-1, keepdims=True) + RMS_EPS) * norm_row
    return y * (gate * jax.nn.sigmoid(gate))


def _hgrn_prompt_kernel(q_ref, f_ref, i_ref, g_ref, lb_ref, norm_ref, o_ref, s_ref, st_sc):
    c = HG_CHUNK
    step = pl.program_id(1)

    @pl.when(step == 0)
    def _():
        st_sc[...] = jnp.zeros_like(st_sc)

    row = lax.broadcasted_iota(jnp.int32, (c, c), 0)
    col = lax.broadcasted_iota(jnp.int32, (c, c), 1)
    tril = (col <= row).astype(_F32)
    sub_row = lax.broadcasted_iota(jnp.int32, (HG_SUB, HG_DK), 0)
    norm_row = norm_ref[...]

    heads = range(HG_HEADS)
    sls = [slice(h * HG_DK, (h + 1) * HG_DK) for h in heads]
    lb = [lb_ref[:, sl] for sl in sls]
    q = [q_ref[0, :, sl] for sl in sls]
    z = [f_ref[0, :, sl] for sl in sls]
    v = [i_ref[0, :, sl] for sl in sls]
    log_f = [jnp.log(lb[h] + (1.0 - lb[h]) * jax.nn.sigmoid(z[h])) for h in heads]
    k = [(1.0 - lb[h]) * jax.nn.sigmoid(-z[h]) for h in heads]
    b = [jnp.dot(tril, log_f[h], preferred_element_type=_F32, precision=lax.Precision.HIGHEST) for h in heads]
    b_last = [b[h][c - 1:c, :] for h in heads]
    st = [st_sc[h] for h in heads]
    v_bf = [v[h].astype(_BF) for h in heads]
    o = [lax.dot_general((q[h] * jnp.exp(b[h])).astype(_BF), st[h].astype(_BF), _NT, preferred_element_type=_F32)
         for h in heads]
    blocks = range(c // HG_SUB)
    rows = [slice(blk * HG_SUB, (blk + 1) * HG_SUB) for blk in blocks]
    a_off = {}
    for blk in blocks[1:]:
        r0 = blk * HG_SUB
        for h in heads:
            ref_row = b[h][r0 - 1:r0, :]
            qd = (q[h][rows[blk]] * jnp.exp(b[h][rows[blk]] - ref_row)).astype(_BF)
            kd = (k[h][:r0] * jnp.exp(ref_row - b[h][:r0])).astype(_BF)
            a_off[blk, h] = lax.dot_general(qd, kd, _NT, preferred_element_type=_F32)
    a_col = {}
    for blk in blocks:
        for s in range(HG_SUB):
            for h in heads:
                bb = b[h][rows[blk]]
                w = jnp.where(sub_row >= s, jnp.exp(bb - bb[s:s + 1, :]), 0.0)
                k_row = k[h][blk * HG_SUB + s:blk * HG_SUB + s + 1, :]
                a_col[blk, s, h] = jnp.sum(q[h][rows[blk]] * k_row * w, axis=-1, keepdims=True)
    off = {(blk, h): jnp.dot(a_off[blk, h].astype(_BF), v_bf[h][:blk * HG_SUB], preferred_element_type=_F32)
           for blk in blocks[1:] for h in heads}
    for h in heads:
        pieces = []
        for blk in blocks:
            ob = o[h][rows[blk]]
            if blk > 0:
                ob = ob + off[blk, h]
            for s in range(HG_SUB):
                ob = ob + a_col[blk, s, h] * v[h][blk * HG_SUB + s:blk * HG_SUB + s + 1, :]
            pieces.append(ob)
        o_ref[0, :, sls[h]] = _hg_out(jnp.concatenate(pieces, axis=0), g_ref[0, :, sls[h]], norm_row)
    st_new = []
    for h in heads:
        kd_all = (k[h] * jnp.exp(b_last[h] - b[h])).astype(_BF)
        st_new.append(st[h] * jnp.exp(b_last[h])
                      + jnp.dot(v[h].T.astype(_BF), kd_all, preferred_element_type=_F32))
        st_sc[h] = st_new[h]

    @pl.when(step == pl.num_programs(1) - 1)
    def _():
        for h in heads:
            s_ref[0, h] = st_new[h].T


def _hgrn_prompt(hg, lb, out_norm):
    bsz, seq, _ = hg.shape
    c = HG_CHUNK
    w = HG_QK_W
    blk = lambda j: pl.BlockSpec((1, c, w), lambda b, l, j=j: (b, l, j))
    return pl.pallas_call(
        _hgrn_prompt_kernel,
        grid=(bsz, seq // c),
        in_specs=[blk(0), blk(1), blk(2), blk(3),
                  pl.BlockSpec((1, w), lambda b, l: (0, 0)),
                  pl.BlockSpec((1, HG_DV), lambda b, l: (0, 0))],
        out_specs=[pl.BlockSpec((1, c, HG_V_W), lambda b, l: (b, l, 0)),
                   pl.BlockSpec((1, HG_HEADS, HG_DK, HG_DV), lambda b, l: (b, 0, 0, 0))],
        out_shape=[jax.ShapeDtypeStruct((bsz, seq, HG_V_W), _F32),
                   jax.ShapeDtypeStruct((bsz, HG_HEADS, HG_DK, HG_DV), _F32)],
        scratch_shapes=[pltpu.VMEM((HG_HEADS, HG_DV, HG_DK), _F32)],
        compiler_params=_params("parallel", "arbitrary"),
        name="hgrn_prompt",
    )(hg, hg, hg, hg, lb.reshape(1, w), out_norm.reshape(1, HG_DV))


def _hgrn_step_kernel(cols_ref, row_ref, lbc_ref, norm_ref, s_ref, o_ref, s_out_ref):
    norm_row = norm_ref[...]
    for h in range(HG_HEADS):
        sl_v = slice(2 * HG_QK_W + h * HG_DV, 2 * HG_QK_W + (h + 1) * HG_DV)
        sl_g = slice(2 * HG_QK_W + HG_V_W + h * HG_DV, 2 * HG_QK_W + HG_V_W + (h + 1) * HG_DV)
        qc = cols_ref[0, h, :, 0:1]
        zc = cols_ref[0, h, :, 1:2]
        lbc = lbc_ref[h]
        f = lbc + (1.0 - lbc) * jax.nn.sigmoid(zc)
        kc = (1.0 - lbc) * jax.nn.sigmoid(-zc)
        v_row = row_ref[0, :, sl_v]
        s_new = f * s_ref[0, h] + kc * v_row
        s_out_ref[0, h] = s_new
        o = jnp.sum(qc * s_new, axis=0, keepdims=True)
        o_ref[0, :, h * HG_DV:(h + 1) * HG_DV] = _hg_out(o, row_ref[0, :, sl_g], norm_row)


def _hgrn_step(hg, state, lb, out_norm):
    bsz = hg.shape[0]
    cols = hg[:, :2 * HG_QK_W].reshape(bsz, 2, HG_HEADS, HG_DK).transpose(0, 2, 3, 1)
    rows = hg.reshape(bsz, 1, HG_IN)
    lbc = lb.reshape(HG_HEADS, HG_DK, 1)
    st_spec = pl.BlockSpec((1, HG_HEADS, HG_DK, HG_DV), lambda b: (b, 0, 0, 0))
    o, s_new = pl.pallas_call(
        _hgrn_step_kernel,
        grid=(bsz,),
        in_specs=[pl.BlockSpec((1, HG_HEADS, HG_DK, 2), lambda b: (b, 0, 0, 0)),
                  pl.BlockSpec((1, 1, HG_IN), lambda b: (b, 0, 0)),
                  pl.BlockSpec((HG_HEADS, HG_DK, 1), lambda b: (0, 0, 0)),
                  pl.BlockSpec((1, HG_DV), lambda b: (0, 0)),
                  st_spec],
        out_specs=[pl.BlockSpec((1, 1, HG_V_W), lambda b: (b, 0, 0)), st_spec],
        out_shape=[jax.ShapeDtypeStruct((bsz, 1, HG_V_W), _F32),
                   jax.ShapeDtypeStruct(state.shape, _F32)],
        compiler_params=_params("parallel"),
        name="hgrn_step",
    )(cols, rows, lbc, out_norm.reshape(1, HG_DV), state)
    return o.reshape(bsz, HG_V_W), s_new


def _rms_rows(x, g):
    return x * lax.rsqrt(jnp.mean(x * x, axis=-1, keepdims=True) + RMS_EPS) * g


def _mla_prep_kernel(x_ref, win_ref, qn_ref, wq_ref, kvn_ref, wk_ref, cos_ref, sin_ref, hm_ref,
                     qcat_ref, kcat_ref, ckv_ref, kpe_ref):
    m = jnp.dot(x_ref[0].astype(_BF), win_ref[...], preferred_element_type=_F32)
    cq = _rms_rows(m[:, :Q_LORA], qn_ref[...])
    c_kv = _rms_rows(m[:, Q_LORA:Q_LORA + KV_LORA], kvn_ref[...])
    kr1 = m[:, Q_LORA + KV_LORA:Q_LORA + KV_LORA + LANES]
    kr2 = m[:, Q_LORA + KV_LORA + LANES:]
    cos = cos_ref[...]
    sin = sin_ref[...]
    k1 = kr1 * cos - kr2 * sin
    k2 = kr1 * sin + kr2 * cos
    ckv_ref[0] = c_kv
    lane = lax.broadcasted_iota(jnp.int32, k1.shape, 1)
    kpe_ref[0] = jnp.where(lane < ROPE_HALF, k1, k2)[:, :ROPE_DIM]
    kcat_ref[0, :, :KV_LORA] = c_kv.astype(_BF)
    kcat_ref[0, :, KV_LORA:KV_LORA + LANES] = k1.astype(_BF)
    kcat_ref[0, :, KV_LORA + LANES:] = k2.astype(_BF)

    qh = jnp.dot(cq.astype(_BF), wq_ref[...], preferred_element_type=_F32)
    nope_w = MLA_HEADS * NOPE_DIM
    q1 = qh[:, nope_w:nope_w + LANES]
    q2 = qh[:, nope_w + LANES:]
    r1 = q1 * cos - q2 * sin
    r2 = q1 * sin + q2 * cos
    for h in range(MLA_HEADS):
        q_nope = qh[:, h * NOPE_DIM:(h + 1) * NOPE_DIM].astype(_BF)
        q_abs = jnp.dot(q_nope, wk_ref[h], preferred_element_type=_F32)
        hm = hm_ref[h:h + 1, :]
        qcat_ref[0, h, :, :KV_LORA] = q_abs.astype(_BF)
        qcat_ref[0, h, :, KV_LORA:KV_LORA + LANES] = (r1 * hm).astype(_BF)
        qcat_ref[0, h, :, KV_LORA + LANES:] = (r2 * hm).astype(_BF)


def _mla_prep(x, w_in, q_norm, w_q, kv_norm, w_k, cos, sin, head_mask, tm=256):
    bsz, seq, d = x.shape
    tm = min(tm, seq)
    full = lambda a: pl.BlockSpec(a.shape, lambda b, l: (0,) * a.ndim)
    qn = q_norm.reshape(1, Q_LORA)
    kvn = kv_norm.reshape(1, KV_LORA)
    return pl.pallas_call(
        _mla_prep_kernel,
        grid=(bsz, seq // tm),
        in_specs=[pl.BlockSpec((1, tm, d), lambda b, l: (b, l, 0)),
                  full(w_in), full(qn), full(w_q), full(kvn), full(w_k),
                  pl.BlockSpec((tm, LANES), lambda b, l: (l, 0)),
                  pl.BlockSpec((tm, LANES), lambda b, l: (l, 0)),
                  full(head_mask)],
        out_specs=[pl.BlockSpec((1, MLA_HEADS, tm, MLA_KW), lambda b, l: (b, 0, l, 0)),
                   pl.BlockSpec((1, tm, MLA_KW), lambda b, l: (b, l, 0)),
                   pl.BlockSpec((1, tm, KV_LORA), lambda b, l: (b, l, 0)),
                   pl.BlockSpec((1, tm, ROPE_DIM), lambda b, l: (b, l, 0))],
        out_shape=[jax.ShapeDtypeStruct((bsz, MLA_HEADS, seq, MLA_KW), _BF),
                   jax.ShapeDtypeStruct((bsz, seq, MLA_KW), _BF),
                   jax.ShapeDtypeStruct((bsz, seq, KV_LORA), _F32),
                   jax.ShapeDtypeStruct((bsz, seq, ROPE_DIM), _F32)],
        compiler_params=_params("parallel", "parallel"),
        name="mla_prep",
    )(x, w_in, qn, w_q, kvn, w_k, cos, sin, head_mask)


def _mla_attn_kernel(q_ref, k_ref, wv_ref, o_ref, m_sc, l_sc, acc_sc, *, tq, tk):
    qi = pl.program_id(1)
    kj = pl.program_id(2)
    rows = MLA_HEADS * tq

    @pl.when(kj == 0)
    def _():
        m_sc[...] = jnp.full_like(m_sc, _NEG_INF)
        l_sc[...] = jnp.zeros_like(l_sc)
        acc_sc[...] = jnp.zeros_like(acc_sc)

    def update(masked):
        q = q_ref[0].reshape(rows, MLA_KW)
        k = k_ref[0]
        s = lax.dot_general(q, k, _NT, preferred_element_type=_F32) * MLA_SCALE
        if masked:
            q_pos = qi * tq + (lax.broadcasted_iota(jnp.int32, (rows, tk), 0) & (tq - 1))
            k_pos = kj * tk + lax.broadcasted_iota(jnp.int32, (rows, tk), 1)
            s = jnp.where(k_pos <= q_pos, s, _NEG_INF)
        m_old = m_sc[...]
        m_new = jnp.maximum(m_old, jnp.max(s, axis=-1, keepdims=True))
        alpha = jnp.exp(m_old - m_new)
        p = jnp.exp(s - m_new)
        l_sc[...] = alpha * l_sc[...] + jnp.sum(p, axis=-1, keepdims=True)
        acc_sc[...] = alpha * acc_sc[...] + jnp.dot(p.astype(_BF), k[:, :KV_LORA], preferred_element_type=_F32)
        m_sc[...] = m_new

    last_key = kj * tk + (tk - 1)

    @pl.when(last_key <= qi * tq)
    def _():
        update(masked=False)

    @pl.when((last_key > qi * tq) & (kj * tk <= qi * tq + (tq - 1)))
    def _():
        update(masked=True)

    @pl.when(kj == pl.num_programs(2) - 1)
    def _():
        o_lat = acc_sc[...] / l_sc[...]
        for h in range(MLA_HEADS):
            oh = o_lat[h * tq:(h + 1) * tq].astype(_BF)
            o_ref[0, :, h * V_DIM:(h + 1) * V_DIM] = jnp.dot(oh, wv_ref[h], preferred_element_type=_F32)


def _mla_prompt_attn(qcat, kcat, w_v, tq=256, tk=512):
    bsz, _, seq, _ = qcat.shape
    tq = min(tq, seq)
    tk = min(tk, seq)
    assert tq & (tq - 1) == 0, "query tile must be a power of two (row -> position uses a bit mask)"

    def k_index(b, i, j):
        return (b, jnp.minimum(j, (i * tq + tq - 1) // tk), 0)

    return pl.pallas_call(
        functools.partial(_mla_attn_kernel, tq=tq, tk=tk),
        grid=(bsz, seq // tq, seq // tk),
        in_specs=[pl.BlockSpec((1, MLA_HEADS, tq, MLA_KW), lambda b, i, j: (b, 0, i, 0)),
                  pl.BlockSpec((1, tk, MLA_KW), k_index),
                  pl.BlockSpec(w_v.shape, lambda b, i, j: (0, 0, 0))],
        out_specs=pl.BlockSpec((1, tq, MLA_HEADS * V_DIM), lambda b, i, j: (b, i, 0)),
        out_shape=jax.ShapeDtypeStruct((bsz, seq, MLA_HEADS * V_DIM), _F32),
        scratch_shapes=[pltpu.VMEM((MLA_HEADS * tq, 1), _F32),
                        pltpu.VMEM((MLA_HEADS * tq, 1), _F32),
                        pltpu.VMEM((MLA_HEADS * tq, KV_LORA), _F32)],
        compiler_params=_params("parallel", "parallel", "arbitrary"),
        name="mla_prompt_attn",
    )(qcat, kcat, w_v)


def _mla_decode_kernel(pt_ref, q_ref, knew_ref, wv_ref, fold_ref, *refs, pages):
    lat_refs = refs[:pages]
    kr_refs = refs[pages:2 * pages]
    o_ref = refs[2 * pages]
    m_sc, l_sc, acc_sc = refs[2 * pages + 1:]
    g = pl.program_id(1)

    @pl.when(g == 0)
    def _():
        m_sc[...] = jnp.full_like(m_sc, _NEG_INF)
        l_sc[...] = jnp.zeros_like(l_sc)
        acc_sc[...] = jnp.zeros_like(acc_sc)

    q = q_ref[0]
    q_abs = q[:, :KV_LORA]
    q_pe = jnp.dot(q[:, KV_LORA:], fold_ref[...], preferred_element_type=_F32).astype(_BF)

    lat = [lat_refs[p][0, 0].astype(_BF) for p in range(pages)]
    s = jnp.concatenate(
        [lax.dot_general(q_abs, lat[p], _NT, preferred_element_type=_F32)
         + jnp.dot(q_pe, kr_refs[p][0].astype(_BF), preferred_element_type=_F32)
         for p in range(pages)], axis=1) * MLA_SCALE
    m_old = m_sc[...]
    m_run = jnp.maximum(m_old, jnp.max(s, axis=-1, keepdims=True))
    alpha = jnp.exp(m_old - m_run)
    pr = jnp.exp(s - m_run)
    l_run = alpha * l_sc[...] + jnp.sum(pr, axis=-1, keepdims=True)
    pr = pr.astype(_BF)
    pv = functools.reduce(jnp.add, [jnp.dot(pr[:, p * PAGE_SIZE:(p + 1) * PAGE_SIZE], lat[p],
                                            preferred_element_type=_F32) for p in range(pages)])
    acc = alpha * acc_sc[...] + pv
    m_sc[...], l_sc[...], acc_sc[...] = m_run, l_run, acc

    @pl.when(g == pl.num_programs(1) - 1)
    def _():
        k_new = knew_ref[0].astype(_F32)
        s_self = jnp.sum(q.astype(_F32) * k_new, axis=-1, keepdims=True) * MLA_SCALE
        m_fin = jnp.maximum(m_run, s_self)
        alpha = jnp.exp(m_run - m_fin)
        p_self = jnp.exp(s_self - m_fin)
        l_fin = alpha * l_run + p_self
        c_new = k_new[:, :KV_LORA]
        o_lat = (alpha * acc + p_self.astype(_BF).astype(_F32) * c_new) / l_fin
        for h in range(MLA_HEADS):
            o_ref[0, :, h * V_DIM:(h + 1) * V_DIM] = jnp.dot(
                o_lat[h:h + 1].astype(_BF), wv_ref[h], preferred_element_type=_F32)


def _mla_decode_attn(qcat, kcat, cache_lat, cache_kr, page_table, w_v, pages=16):
    bsz = qcat.shape[0]
    n_pages = page_table.shape[1]
    lat4 = cache_lat.reshape((1,) + cache_lat.shape)
    kr3 = jnp.swapaxes(cache_kr, 1, 2)

    def lat_spec(p):
        return pl.BlockSpec((1, 1, PAGE_SIZE, KV_LORA), lambda b, g, pt: (0, pt[b, g * pages + p], 0, 0))

    def kr_spec(p):
        return pl.BlockSpec((1, ROPE_DIM, PAGE_SIZE), lambda b, g, pt: (pt[b, g * pages + p], 0, 0))

    lane = jnp.arange(2 * LANES)
    dst = (lane // LANES) * ROPE_HALF + lane % ROPE_HALF
    fold = (jnp.arange(ROPE_DIM)[None, :] == dst[:, None]).astype(_BF)
    in_specs = [pl.BlockSpec((1, MLA_HEADS, MLA_KW), lambda b, g, pt: (b, 0, 0)),
                pl.BlockSpec((1, 1, MLA_KW), lambda b, g, pt: (b, 0, 0)),
                pl.BlockSpec(w_v.shape, lambda b, g, pt: (0, 0, 0)),
                pl.BlockSpec(fold.shape, lambda b, g, pt: (0, 0))]
    in_specs += [lat_spec(p) for p in range(pages)]
    in_specs += [kr_spec(p) for p in range(pages)]
    out = pl.pallas_call(
        functools.partial(_mla_decode_kernel, pages=pages),
        grid_spec=pltpu.PrefetchScalarGridSpec(
            num_scalar_prefetch=1,
            grid=(bsz, n_pages // pages),
            in_specs=in_specs,
            out_specs=pl.BlockSpec((1, 1, MLA_HEADS * V_DIM), lambda b, g, pt: (b, 0, 0)),
            scratch_shapes=[pltpu.VMEM((MLA_HEADS, 1), _F32),
                            pltpu.VMEM((MLA_HEADS, 1), _F32),
                            pltpu.VMEM((MLA_HEADS, KV_LORA), _F32)]),
        out_shape=jax.ShapeDtypeStruct((bsz, 1, MLA_HEADS * V_DIM), _F32),
        compiler_params=_params("parallel", "arbitrary"),
        name="mla_decode_attn",
    )(page_table, qcat, kcat, w_v, fold, *([lat4] * pages), *([kr3] * pages))
    return out.reshape(bsz, MLA_HEADS * V_DIM)


_CONV_HALO = 32
_CONV_ROWS = 32


def _conv_prompt_kernel(u_ref, wdw_ref, bdw_ref, g_ref, b_ref, o_ref, ext_sc, sh_sc, *, tl):
    @pl.when(pl.program_id(1) == 0)
    def _():
        ext_sc[:_CONV_HALO, :] = jnp.zeros((_CONV_HALO, CONV_DIM), _F32)

    ext_sc[_CONV_HALO:, :] = u_ref[0]
    first = _CONV_HALO - (CONV_WIDTH - 1)
    span = sh_sc.shape[1]
    for r in range(1, SUBLANES):
        sh_sc[r - 1] = ext_sc[r:r + span, :]
    bias, gain, shift = bdw_ref[...], g_ref[...], b_ref[...]
    for c0 in range(0, tl, _CONV_ROWS):
        y = jnp.zeros((_CONV_ROWS, CONV_DIM), _F32) + bias
        for w in range(CONV_WIDTH):
            r = (first + w) % SUBLANES
            a = first + w - r + c0
            src = ext_sc[a:a + _CONV_ROWS, :] if r == 0 else sh_sc[r - 1, a:a + _CONV_ROWS, :]
            y = y + src * wdw_ref[w:w + 1, :]
        y = _layer_norm_rows(y, gain, shift)
        o_ref[0, c0:c0 + _CONV_ROWS, :] = y * jax.nn.sigmoid(y)
    ext_sc[:_CONV_HALO, :] = ext_sc[tl:tl + _CONV_HALO, :]


def _conv_prompt(u, w_dw, b_dw, ln_g, ln_b, tl=256):
    bsz, seq, d = u.shape
    tl = min(tl, seq)
    vec = pl.BlockSpec((1, d), lambda b, l: (0, 0))
    return pl.pallas_call(
        functools.partial(_conv_prompt_kernel, tl=tl),
        grid=(bsz, seq // tl),
        in_specs=[pl.BlockSpec((1, tl, d), lambda b, l: (b, l, 0)),
                  pl.BlockSpec((CONV_WIDTH, d), lambda b, l: (0, 0)), vec, vec, vec],
        out_specs=pl.BlockSpec((1, tl, d), lambda b, l: (b, l, 0)),
        out_shape=jax.ShapeDtypeStruct((bsz, seq, d), _F32),
        scratch_shapes=[pltpu.VMEM((_CONV_HALO + tl, d), _F32),
                        pltpu.VMEM((SUBLANES - 1, tl + _CONV_HALO - SUBLANES, d), _F32)],
        compiler_params=_params("parallel", "arbitrary"),
        name="conv_prompt",
    )(u, w_dw, b_dw.reshape(1, d), ln_g.reshape(1, d), ln_b.reshape(1, d))


def _conv_step_kernel(u_ref, st_ref, wdw_ref, bdw_ref, g_ref, b_ref, o_ref, st_out_ref):
    st = st_ref[0]
    u = u_ref[0]
    hist = CONV_WIDTH - 1
    y = (jnp.sum(st * wdw_ref[:hist, :], axis=0, keepdims=True) + u * wdw_ref[hist:CONV_WIDTH, :]
         + bdw_ref[...])
    y = _layer_norm_rows(y, g_ref[...], b_ref[...])
    o_ref[0] = y * jax.nn.sigmoid(y)
    st_out_ref[0, :hist - 1, :] = st[1:, :]
    st_out_ref[0, hist - 1:, :] = u


def _conv_step(u, state, w_dw, b_dw, ln_g, ln_b):
    bsz, d = u.shape
    hist = CONV_WIDTH - 1
    vec = pl.BlockSpec((1, d), lambda b: (0, 0))
    y, st_new = pl.pallas_call(
        _conv_step_kernel,
        grid=(bsz,),
        in_specs=[pl.BlockSpec((1, 1, d), lambda b: (b, 0, 0)),
                  pl.BlockSpec((1, hist, d), lambda b: (b, 0, 0)),
                  pl.BlockSpec((CONV_WIDTH, d), lambda b: (0, 0)), vec, vec, vec],
        out_specs=[pl.BlockSpec((1, 1, d), lambda b: (b, 0, 0)),
                   pl.BlockSpec((1, hist, d), lambda b: (b, 0, 0))],
        out_shape=[jax.ShapeDtypeStruct((bsz, 1, d), _F32),
                   jax.ShapeDtypeStruct((bsz, hist, d), _F32)],
        compiler_params=_params("parallel"),
        name="conv_step",
    )(u.reshape(bsz, 1, d), state, w_dw, b_dw.reshape(1, d), ln_g.reshape(1, d), ln_b.reshape(1, d))
    return y.reshape(bsz, d), st_new


def _oddeven_merge(lo, hi, r):
    step = r * 2
    if step < hi - lo:
        yield from _oddeven_merge(lo, hi, step)
        yield from _oddeven_merge(lo + r, hi, step)
        yield from [(i, i + r) for i in range(lo + r, hi - r, step)]
    else:
        yield (lo, lo + r)


def _oddeven_sort(lo, hi):
    if hi - lo >= 1:
        mid = lo + (hi - lo) // 2
        yield from _oddeven_sort(lo, mid)
        yield from _oddeven_sort(mid + 1, hi)
        yield from _oddeven_merge(lo, hi, 1)


_SORT_NET = tuple(_oddeven_sort(0, PEER_TOPK - 1))
_MERGE_NET = tuple((i, i + d) for d in (8, 4, 2, 1) for i in range(PEER_TOPK) if i & d == 0)


def _exchange(x, net):
    for i, j in net:
        x[i], x[j] = jnp.maximum(x[i], x[j]), jnp.minimum(x[i], x[j])


def _merge_sublanes(x):
    for shift in (4, 2, 1):
        x = [jnp.maximum(x[i], pltpu.roll(x[PEER_TOPK - 1 - i], shift, 0)) for i in range(PEER_TOPK)]
        _exchange(x, _MERGE_NET)
    return x


def _top16(s):
    x = [s[SUBLANES * v:SUBLANES * (v + 1), :] for v in range(PEER_TOPK)]
    _exchange(x, _SORT_NET)
    return _merge_sublanes(x)


def _next_value(s, last):
    n_ge = jnp.sum((s >= last).astype(_F32), axis=0, keepdims=True)
    below = jnp.max(jnp.where(s < last, s, _NEG_INF), axis=0, keepdims=True)
    return jnp.where(n_ge > PEER_TOPK, last, below)


def _peer_route_kernel(x_ref, wq_ref, keys_ref, th_ref, ea_ref, eb_ref):
    xb = x_ref[...].astype(_BF)
    q_t = lax.dot_general(wq_ref[...], xb, _NT, preferred_element_type=_F32)
    tm = xb.shape[0]
    sub = lax.broadcasted_iota(jnp.int32, (SUBLANES, tm), 0)
    for h in range(PEER_HEADS):
        scores, tops = [], []
        for p in range(2):
            r0 = (h * 2 + p) * PEER_HALF
            q_hp = q_t[r0:r0 + PEER_HALF, :].astype(_BF)
            s = jnp.dot(keys_ref[p, h], q_hp, preferred_element_type=_F32)
            scores.append(s)
            tops.append(_top16(s))
        s1, s2 = scores
        y1, y2 = tops
        v1_lo, v1_hi = y1[0], y1[SUBLANES]
        for a in range(1, SUBLANES):
            v1_lo = jnp.where(sub == a, y1[a], v1_lo)
            v1_hi = jnp.where(sub == a, y1[SUBLANES + a], v1_hi)
        cells = []
        for b in range(PEER_TOPK):
            c = v1_lo + y2[b]
            lim = PEER_TOPK // (b + 1)
            cells.append(c if lim >= SUBLANES else jnp.where(sub < lim, c, _NEG_INF))
        extra = v1_hi + y2[0]
        x = list(cells)
        x[PEER_TOPK - 1] = jnp.maximum(x[PEER_TOPK - 1], extra)
        _exchange(x, _MERGE_NET)
        tau = _merge_sublanes(x)[PEER_TOPK - 1]
        m1, m2 = y1[0], y2[0]
        e1_lo = jnp.exp(v1_lo - m1)
        z = jnp.where(extra >= tau, jnp.exp(v1_hi - m1), 0.0)
        for b in range(PEER_TOPK):
            z = z + jnp.where(cells[b] >= tau, e1_lo * jnp.exp(y2[b] - m2), 0.0)
        z = jnp.sum(z, axis=0, keepdims=True)
        ea = jnp.exp(s1 - m1[0:1, :]) / z
        eb = jnp.exp(s2 - m2[0:1, :])
        tau_row = tau[0:1, :]
        n_ge = functools.reduce(jnp.add, [(c >= tau).astype(_F32) for c in cells + [extra]])
        below = functools.reduce(jnp.maximum, [jnp.where(c < tau, c, _NEG_INF) for c in cells + [extra]])
        c17 = jnp.where(jnp.sum(n_ge, axis=0, keepdims=True) > PEER_TOPK, tau_row,
                        jnp.max(below, axis=0, keepdims=True))
        c17 = jnp.maximum(c17, jnp.maximum(m1[0:1, :] + _next_value(s2, y2[PEER_TOPK - 1][0:1, :]),
                                           m2[0:1, :] + _next_value(s1, y1[PEER_TOPK - 1][0:1, :])))
        theta = jnp.exp(0.5 * (tau_row + c17) - s1 - m2[0:1, :])
        for cb in range(tm // LANES):
            cols = slice(cb * LANES, (cb + 1) * LANES)
            th_ref[h, cb] = theta[:, cols]
            ea_ref[h, cb] = ea[:, cols]
            eb_ref[h, cb] = eb[:, cols]


def _peer_expert_kernel(xt_ref, u_ref, vt_ref, th_ref, ea_ref, eb_ref, o_ref,
                        h0_sc, h1_sc, p0_sc, p1_sc, acc_sc, *, te, ne):
    g = pl.program_id(0)
    e_c = jnp.maximum(g - 2, 0) % ne

    @pl.when(g == 0)
    def _():
        for ref in (h0_sc, h1_sc, p0_sc, p1_sc):
            ref[...] = jnp.zeros_like(ref)

    @pl.when((g < 2) | (e_c == 0))
    def _():
        acc_sc[...] = jnp.zeros_like(acc_sc)

    groups = N_KEYS // SUBLANES

    def key_rows(ii, n_rows):
        return pl.ds(ii * n_rows, n_rows)

    def gate_tile(h_b, p_b, ii, cb):
        w = [jnp.zeros((SUBLANES, LANES), _F32) for _ in range(groups)]
        for h in range(PEER_HEADS):
            th = jnp.broadcast_to(th_ref[h, cb, pl.ds(ii, 1), :], (SUBLANES, LANES))
            ea = jnp.broadcast_to(ea_ref[h, cb, pl.ds(ii, 1), :], (SUBLANES, LANES))
            for j in range(groups):
                eb = eb_ref[h, cb, j * SUBLANES:(j + 1) * SUBLANES, :]
                w[j] = w[j] + jnp.where(eb >= th, ea * eb, 0.0)
            yield
        rows = key_rows(ii, N_KEYS)
        hh = h_b[cb, rows, :]
        act = 0.5 * hh * (1.0 + lax.erf(hh * (2.0 ** -0.5)))
        p_b[cb, rows, :] = (jnp.concatenate(w, axis=0) * act).astype(_BF)
        yield

    def step(h_a, h_b, p_b, p_c):
        n_cb = h_b.shape[0]
        per = min(MXU_WIDTH // LANES, n_cb)
        n_ii = te // N_KEYS
        d_rows = acc_sc.shape[0] // n_ii

        def stage_c(ii, cbs, cols):
            out_rows = key_rows(ii, d_rows)
            p_grp = jnp.concatenate([p_c[cb] for cb in cbs], axis=1)
            acc_sc[out_rows, cols] += jnp.dot(vt_ref[0, out_rows, :], p_grp, preferred_element_type=_F32)

        def stage_a(ii, cbs, cols):
            rows = key_rows(ii, N_KEYS)
            hv = jnp.dot(u_ref[rows, :], xt_ref[0, :, cols], preferred_element_type=_F32)
            for k, cb in enumerate(cbs):
                h_a[cb, rows, :] = hv[:, k * LANES:(k + 1) * LANES]

        for ii in range(n_ii):
            for grp in range(n_cb // per):
                cbs = range(grp * per, (grp + 1) * per)
                cols = slice(grp * per * LANES, (grp + 1) * per * LANES)
                pieces = [functools.partial(f, ii, cbs, cols) for f in (stage_c, stage_a)]
                chunks = itertools.chain.from_iterable(gate_tile(h_b, p_b, ii, cb) for cb in cbs)
                n_chunks = len(cbs) * (PEER_HEADS + 1)
                every = max(n_chunks // len(pieces), 1)
                for idx in range(n_chunks):
                    if idx % every == 0 and pieces:
                        pieces.pop(0)()
                    next(chunks)
                for piece in pieces:
                    piece()

    @pl.when(g % 2 == 0)
    def _():
        step(h0_sc, h1_sc, p1_sc, p0_sc)

    @pl.when(g % 2 == 1)
    def _():
        step(h1_sc, h0_sc, p0_sc, p1_sc)

    @pl.when((g >= 2) & (e_c == ne - 1))
    def _():
        o_ref[...] = acc_sc[...].T


def _peer(x, wq_t, keys, u_bf, v_tiles, tm=512, tr=256):
    n, d = x.shape
    te = v_tiles.shape[2]
    tm = min(tm, n)
    tr = min(tr, n)
    rows = PEER_HEADS * 2 * PEER_HALF
    sc_spec = pl.BlockSpec((PEER_HEADS, tr // LANES, N_KEYS, LANES), lambda t: (0, t, 0, 0))
    sc_shape = jax.ShapeDtypeStruct((PEER_HEADS, n // LANES, N_KEYS, LANES), _F32)
    th, ea, eb = pl.pallas_call(
        _peer_route_kernel,
        grid=(n // tr,),
        in_specs=[pl.BlockSpec((tr, d), lambda t: (t, 0)),
                  pl.BlockSpec((rows, d), lambda t: (0, 0)),
                  pl.BlockSpec(keys.shape, lambda t: (0, 0, 0, 0))],
        out_specs=[sc_spec, sc_spec, sc_spec],
        out_shape=[sc_shape, sc_shape, sc_shape],
        compiler_params=_params("parallel"),
        name="peer_route",
    )(x, wq_t, keys)

    nt = n // tm
    ne = N_EXPERTS // te
    last = nt * ne - 1
    step_a = lambda g: jnp.minimum(g, last)
    tile_b = lambda g: jnp.minimum(jnp.maximum(g - 1, 0) // ne, nt - 1)
    step_c = lambda g: jnp.maximum(g - 2, 0)
    assert (te // N_KEYS) % SUBLANES == 0, "first-half key rows of an expert tile must fill whole sublane tiles"
    n_cb = tm // LANES
    xt_tiles = x.astype(_BF).reshape(nt, tm, d).transpose(0, 2, 1)
    sc_all = pl.BlockSpec((PEER_HEADS, n_cb, N_KEYS, LANES), lambda g: (0, tile_b(g), 0, 0))
    sc_rows = pl.BlockSpec((PEER_HEADS, n_cb, te // N_KEYS, LANES),
                           lambda g: (0, tile_b(g), jnp.maximum(g - 1, 0) % ne, 0))
    return pl.pallas_call(
        functools.partial(_peer_expert_kernel, te=te, ne=ne),
        grid=(nt * ne + 2,),
        in_specs=[pl.BlockSpec((1, d, tm), lambda g: (step_a(g) // ne, 0, 0)),
                  pl.BlockSpec((te, d), lambda g: (step_a(g) % ne, 0)),
                  pl.BlockSpec((1, d, te), lambda g: (step_c(g) % ne, 0, 0)),
                  sc_rows, sc_rows, sc_all],
        out_specs=pl.BlockSpec((tm, d), lambda g: (step_c(g) // ne, 0)),
        out_shape=jax.ShapeDtypeStruct((n, d), _F32),
        scratch_shapes=[pltpu.VMEM((n_cb, te, LANES), _F32), pltpu.VMEM((n_cb, te, LANES), _F32),
                        pltpu.VMEM((n_cb, te, LANES), _BF), pltpu.VMEM((n_cb, te, LANES), _BF),
                        pltpu.VMEM((d, tm), _F32)],
        compiler_params=_params("arbitrary"),
        name="peer_experts",
    )(xt_tiles, u_bf, v_tiles, th, ea, eb)


def _rope_tables(pos):
    inv_freq = ROPE_THETA ** (-jnp.arange(ROPE_HALF, dtype=_F32) / ROPE_HALF)
    ang = pos.astype(_F32)[:, None] * inv_freq[None, :]
    return jnp.tile(jnp.cos(ang), (1, MLA_HEADS)), jnp.tile(jnp.sin(ang), (1, MLA_HEADS))


def _mixer_weights(w_in, w_uq, w_ukv, w_out):
    w_hg = w_in[:, :HG_IN].astype(_BF)
    o = HG_IN
    cq_w = w_in[:, o:o + Q_LORA]
    ckv_w = w_in[:, o + Q_LORA:o + Q_LORA + KV_LORA]
    kr_w = w_in[:, o + Q_LORA + KV_LORA:]
    w_mla = jnp.concatenate([cq_w, ckv_w, jnp.tile(kr_w[:, :ROPE_HALF], (1, MLA_HEADS)),
                             jnp.tile(kr_w[:, ROPE_HALF:], (1, MLA_HEADS))], axis=1).astype(_BF)
    wq = w_uq.reshape(Q_LORA, MLA_HEADS, NOPE_DIM + ROPE_DIM)
    w_q = jnp.concatenate([wq[:, :, :NOPE_DIM].reshape(Q_LORA, -1),
                           wq[:, :, NOPE_DIM:NOPE_DIM + ROPE_HALF].reshape(Q_LORA, -1),
                           wq[:, :, NOPE_DIM + ROPE_HALF:].reshape(Q_LORA, -1)], axis=1).astype(_BF)
    w_k = jnp.transpose(w_ukv[:, :, :NOPE_DIM], (1, 2, 0)).astype(_BF)
    w_v = jnp.transpose(w_ukv[:, :, NOPE_DIM:], (1, 0, 2)).astype(_BF)
    return w_hg, w_mla, w_q, w_k, w_v, w_out[:HG_V_W].astype(_BF), w_out[HG_V_W:].astype(_BF)


def kernel(x_prompt, x_sample, cache_mla_latent, cache_mla_krope, state_hgrn, state_conv, page_table, hg_lb_logits, mix_w_in, mla_q_norm, mla_w_uq, mla_kv_norm, mla_w_ukv, hg_out_norm, mix_w_out, conv_w_pw1, conv_b_pw1, conv_w_dw, conv_b_dw, conv_ln_g, conv_ln_b, conv_w_pw2, conv_b_pw2, peer_w_q, peer_sub_keys, peer_u, peer_v, ln_g, ln_b):
    bp, lp, d = x_prompt.shape
    bs, ls, _ = x_sample.shape
    assert ls == 1, "the sample group decodes one token per sequence"
    past_len = page_table.shape[1] * PAGE_SIZE
    lb_all = jnp.cumsum(jax.nn.softmax(hg_lb_logits.astype(_F32), axis=0), axis=0)
    head_mask = (jnp.arange(LANES)[None, :] // ROPE_HALF == jnp.arange(MLA_HEADS)[:, None]).astype(_F32)
    cos_p, sin_p = _rope_tables(jnp.arange(lp))
    cos_s, sin_s = _rope_tables(jnp.full((bs,), past_len))

    hp = x_prompt.reshape(bp * lp, d)
    hs = x_sample.reshape(bs, d)
    outs_p = {"lat": [], "kpe": [], "hg": [], "conv": []}
    outs_s = {"lat": [], "kpe": [], "hg": [], "conv": []}
    for layer in range(DEPTH):
        if layer % 2 == 0:
            e = layer // 2
            lb = lb_all[layer]
            w_hg, w_mla, w_q, w_k, w_v, w_out_hg, w_out_mla = _mixer_weights(
                mix_w_in[e], mla_w_uq[e], mla_w_ukv[e], mix_w_out[e])
            hg = _linear([hp], [w_hg]).reshape(bp, lp, HG_IN)
            o_hg, s_new = _hgrn_prompt(hg, lb, hg_out_norm[e])
            qcat, kcat, c_kv, k_pe = _mla_prep(hp.reshape(bp, lp, d), w_mla, mla_q_norm[e], w_q, mla_kv_norm[e],
                                               w_k, cos_p, sin_p, head_mask)
            o_mla = _mla_prompt_attn(qcat, kcat, w_v)
            hp_mix = _linear_ln([o_hg.reshape(bp * lp, HG_V_W), o_mla.reshape(bp * lp, MLA_HEADS * V_DIM)],
                                [w_out_hg, w_out_mla], None, hp, ln_g[layer, 0], ln_b[layer, 0])
            outs_p["lat"].append(c_kv)
            outs_p["kpe"].append(k_pe)
            outs_p["hg"].append(s_new)
            hg = _linear([hs], [w_hg])
            o_hg, s_new = _hgrn_step(hg, state_hgrn[e], lb, hg_out_norm[e])
            qcat, kcat, c_kv, k_pe = _mla_prep(hs.reshape(1, bs, d), w_mla, mla_q_norm[e], w_q, mla_kv_norm[e],
                                               w_k, cos_s, sin_s, head_mask)
            o_mla = _mla_decode_attn(jnp.transpose(qcat[0], (1, 0, 2)), kcat.reshape(bs, 1, MLA_KW),
                                     cache_mla_latent[e], cache_mla_krope[e], page_table, w_v)
            ms = _linear([o_hg, o_mla], [w_out_hg, w_out_mla])
            outs_s["lat"].append(c_kv.reshape(bs, 1, KV_LORA))
            outs_s["kpe"].append(k_pe.reshape(bs, 1, ROPE_DIM))
            outs_s["hg"].append(s_new)
        else:
            o = layer // 2
            w1 = conv_w_pw1[o].astype(_BF)
            w2 = conv_w_pw2[o].astype(_BF)
            u = _glu_linear(hp, w1, conv_b_pw1[o]).reshape(bp, lp, CONV_DIM)
            y = _conv_prompt(u, conv_w_dw[o], conv_b_dw[o], conv_ln_g[o], conv_ln_b[o])
            hp_mix = _linear_ln([y.reshape(bp * lp, CONV_DIM)], [w2], conv_b_pw2[o], hp,
                                ln_g[layer, 0], ln_b[layer, 0])
            outs_p["conv"].append(u[:, lp - (CONV_WIDTH - 1):, :])
            u = _glu_linear(hs, w1, conv_b_pw1[o])
            y, st_new = _conv_step(u, state_conv[o], conv_w_dw[o], conv_b_dw[o], conv_ln_g[o], conv_ln_b[o])
            ms = _linear([y], [w2], conv_b_pw2[o])
            outs_s["conv"].append(st_new)
        hp = hp_mix
        hs = _add_ln(hs, ms, ln_g[layer, 0], ln_b[layer, 0])
        wq_t = peer_w_q[layer].T.astype(_BF)
        keys = peer_sub_keys[layer].astype(_BF)
        u_bf = peer_u[layer].astype(_BF)
        v_tiles = peer_v[layer].astype(_BF).reshape(N_EXPERTS // PEER_TE, PEER_TE, d).transpose(0, 2, 1)
        hp = _add_ln(hp, _peer(hp, wq_t, keys, u_bf, v_tiles), ln_g[layer, 1], ln_b[layer, 1])
        hs = _add_ln(hs, _peer(hs, wq_t, keys, u_bf, v_tiles), ln_g[layer, 1], ln_b[layer, 1])
    stack = lambda xs: jnp.stack(xs)
    return (hp.reshape(bp, lp, d), hs.reshape(bs, ls, d),
            stack(outs_p["lat"]), stack(outs_p["kpe"]), stack(outs_p["hg"]), stack(outs_p["conv"]),
            stack(outs_s["lat"]), stack(outs_s["kpe"]), stack(outs_s["hg"]), stack(outs_s["conv"]))
```
